```python
import math
import jax, jax.numpy as jnp
from jax import lax
import numpy as np


D_MODEL = 2048
BATCH = 4
SEQ = 2048
DEPTH = 1

MEM_LEN = 256
ML_HEADS = 4
ML_QK = 256
ML_V = 512
ML_CHUNK = 64
CONV_W = 4
ML_QK_W = ML_HEADS * ML_QK
ML_V_W = ML_HEADS * ML_V
FOX_HEADS = 16
FOX_HD = 128
FOX_W = FOX_HEADS * FOX_HD
Q_BLOCK = 128
X_HEADS = 4
X_HD = D_MODEL // X_HEADS
PEER_HEADS = 8
N_KEYS = 128
N_EXPERTS = N_KEYS * N_KEYS
PEER_DK = 256
PEER_TOPK = 16
PEER_TOK_BLOCK = 128
LN_EPS = 1e-5
DN_ALPHA = (2 * DEPTH) ** 0.25
DN_BETA = (8 * DEPTH) ** -0.25
SPLIT_SIZES = (ML_QK_W, ML_QK_W, ML_V_W, ML_V_W, ML_HEADS, ML_HEADS,
               FOX_W, FOX_W, FOX_W, FOX_HEADS, D_MODEL, D_MODEL)
IN_PROJ_W = sum(SPLIT_SIZES)

kernel_name = 'hybrid_mlstm_fox_peer_deepnorm'

F32 = jnp.float32


def layer_norm(x, w, b):
    xf = x.astype(F32)
    mu = jnp.mean(xf, -1, keepdims=True)
    var = jnp.mean(jnp.square(xf - mu), -1, keepdims=True)
    return ((xf - mu) * lax.rsqrt(var + LN_EPS) * w.astype(F32) + b.astype(F32)).astype(x.dtype)


def _heads(t, n):
    B, S, _ = t.shape
    return t.reshape(B, S, n, -1).transpose(0, 2, 1, 3)


def causal_dwconv(x, w, b):
    C = x.shape[-1]
    y = lax.conv_general_dilated(x, w[:, None, :].astype(x.dtype), window_strides=(1,),
                                 padding=((CONV_W - 1, 0),), dimension_numbers=('NWC', 'WIO', 'NWC'),
                                 feature_group_count=C)
    return y + b.astype(x.dtype)


def mlstm_chunkwise(q, k, v, i_pre, f_pre):
    B, H, S, dk = q.shape
    dv = v.shape[-1]
    L = ML_CHUNK
    nc = S // L
    q = q.astype(F32).reshape(B, H, nc, L, dk)
    k = (k.astype(F32) * dk ** -0.5).reshape(B, H, nc, L, dk)
    v = v.astype(F32).reshape(B, H, nc, L, dv)
    ig = i_pre.reshape(B, H, nc, L)
    b = jnp.cumsum(jax.nn.log_sigmoid(f_pre).reshape(B, H, nc, L), -1)
    b_last = b[..., -1]
    w_end = b_last[..., None] - b + ig
    m_end_intra = jnp.max(w_end, -1)

    def step(carry, inp):
        C, n, m = carry
        k_c, v_c, w_c, bl_c, mi_c = inp
        m_new = jnp.maximum(bl_c + m, mi_c)
        decay = jnp.exp(bl_c + m - m_new)
        wk = jnp.exp(w_c - m_new[..., None])[..., None] * k_c
        C_new = decay[..., None, None] * C + jnp.einsum('bhld,bhle->bhde', wk, v_c)
        n_new = decay[..., None] * n + jnp.sum(wk, -2)
        return (C_new, n_new, m_new), (C, n, m)

    init = (jnp.zeros((B, H, dk, dv), F32), jnp.zeros((B, H, dk), F32), jnp.zeros((B, H), F32))
    xs = (jnp.moveaxis(k, 2, 0), jnp.moveaxis(v, 2, 0), jnp.moveaxis(w_end, 2, 0),
          jnp.moveaxis(b_last, 2, 0), jnp.moveaxis(m_end_intra, 2, 0))
    _, (C_prev, n_prev, m_prev) = lax.scan(step, init, xs)
    C_prev = jnp.moveaxis(C_prev, 0, 2)
    n_prev = jnp.moveaxis(n_prev, 0, 2)
    m_prev = jnp.moveaxis(m_prev, 0, 2)

    causal = jnp.tril(jnp.ones((L, L), dtype=bool))
    Dlog = jnp.where(causal, b[..., :, None] - b[..., None, :] + ig[..., None, :], -jnp.inf)
    a = b + m_prev[..., None]
    m_t = jnp.maximum(a, jnp.max(Dlog, -1))
    P = jnp.exp(Dlog - m_t[..., None]) * jnp.einsum('bhctd,bhcsd->bhcts', q, k)
    inter = jnp.exp(a - m_t)
    num = jnp.einsum('bhcts,bhcse->bhcte', P, v) + inter[..., None] * jnp.einsum('bhctd,bhcde->bhcte', q, C_prev)
    den = jnp.sum(P, -1) + inter * jnp.einsum('bhctd,bhcd->bhct', q, n_prev)
    h = num / jnp.maximum(jnp.abs(den), jnp.exp(-m_t))[..., None]
    return h.reshape(B, H, S, dv)


def forgetting_attention(q, k, v, f_pre):
    B, H, S, d = q.shape
    F = jnp.cumsum(jax.nn.log_sigmoid(f_pre), -1)
    nb = S // Q_BLOCK
    qb = q.reshape(B, H, nb, Q_BLOCK, d).transpose(2, 0, 1, 3, 4)
    Fqb = F.reshape(B, H, nb, Q_BLOCK).transpose(2, 0, 1, 3)
    pos_k = jnp.arange(S)
    scale = d ** -0.5

    def block(args):
        qi, Fi, start = args
        s = jnp.einsum('bhqd,bhkd->bhqk', qi, k, preferred_element_type=F32) * scale
        s = s + Fi[..., :, None] - F[..., None, :]
        pos_q = start + jnp.arange(Q_BLOCK)
        s = jnp.where(pos_k[None, :] <= pos_q[:, None], s, -jnp.inf)
        p = jax.nn.softmax(s, -1)
        return jnp.einsum('bhqk,bhkd->bhqd', p.astype(v.dtype), v)

    out = lax.map(block, (qb, Fqb, jnp.arange(nb) * Q_BLOCK))
    return out.transpose(1, 2, 0, 3, 4).reshape(B, H, S, d)


def hybrid_mixer(x, w_in, conv_w, conv_b, gate_b, norm_w, fox_f_b, w_br_ml, w_br_fox, w_out):
    B, S, _ = x.shape
    cuts = [int(c) for c in np.cumsum(SPLIT_SIZES)[:-1]]
    (ml_q, ml_k, ml_v, ml_o, ml_i, ml_f,
     fx_q, fx_k, fx_v, fx_f, g_ml, g_fx) = jnp.split(x @ w_in, cuts, axis=-1)
    qk = jax.nn.silu(causal_dwconv(jnp.concatenate([ml_q, ml_k], -1), conv_w, conv_b))
    ml_q, ml_k = qk[..., :ML_QK_W], qk[..., ML_QK_W:]
    i_pre = (ml_i.astype(F32) + gate_b[0].astype(F32)).transpose(0, 2, 1)
    f_pre = (ml_f.astype(F32) + gate_b[1].astype(F32)).transpose(0, 2, 1)
    h = mlstm_chunkwise(_heads(ml_q, ML_HEADS), _heads(ml_k, ML_HEADS), _heads(ml_v, ML_HEADS), i_pre, f_pre)
    mu = jnp.mean(h, -1, keepdims=True)
    var = jnp.mean(jnp.square(h - mu), -1, keepdims=True)
    h = (h - mu) * lax.rsqrt(var + LN_EPS) * norm_w.astype(F32).reshape(ML_HEADS, 1, ML_V)
    h = h.transpose(0, 2, 1, 3).reshape(B, S, ML_V_W).astype(x.dtype) * jax.nn.sigmoid(ml_o)
    fox_f_pre = (fx_f.astype(F32) + fox_f_b.astype(F32)).transpose(0, 2, 1)
    y = forgetting_attention(_heads(fx_q, FOX_HEADS), _heads(fx_k, FOX_HEADS), _heads(fx_v, FOX_HEADS), fox_f_pre)
    y = y.transpose(0, 2, 1, 3).reshape(B, S, FOX_W)
    merged = jax.nn.sigmoid(g_ml) * (h @ w_br_ml) + jax.nn.sigmoid(g_fx) * (y @ w_br_fox)
    return merged @ w_out


def memory_cross_attention(x, mem, w_q, w_kv, w_o):
    B, S, _ = x.shape
    q = (x @ w_q).reshape(B, S, X_HEADS, X_HD)
    kv = mem @ w_kv
    k = kv[..., :D_MODEL].reshape(B, -1, X_HEADS, X_HD)
    v = kv[..., D_MODEL:].reshape(B, -1, X_HEADS, X_HD)
    s = jnp.einsum('bshd,bmhd->bhsm', q, k, preferred_element_type=F32) * X_HD ** -0.5
    p = jax.nn.softmax(s, -1)
    o = jnp.einsum('bhsm,bmhd->bshd', p.astype(v.dtype), v).reshape(B, S, D_MODEL)
    return o @ w_o


def peer_ffn(x, w_pq, sub_keys, expert_u, expert_v):
    B, S, D = x.shape
    q = (x @ w_pq).reshape(B, S, PEER_HEADS, 2, PEER_DK // 2)
    sc = jnp.einsum('bshpd,hpnd->bshpn', q, sub_keys, preferred_element_type=F32)
    top_s, top_i = lax.top_k(sc, PEER_TOPK)
    cand_s = (top_s[..., 0, :, None] + top_s[..., 1, None, :]).reshape(B, S, PEER_HEADS, PEER_TOPK * PEER_TOPK)
    cand_i = (top_i[..., 0, :, None] * N_KEYS + top_i[..., 1, None, :]).reshape(B, S, PEER_HEADS, PEER_TOPK * PEER_TOPK)
    best_s, best_pos = lax.top_k(cand_s, PEER_TOPK)
    idx = jnp.take_along_axis(cand_i, best_pos, -1)
    g = jax.nn.softmax(best_s, -1)
    nblk = (B * S) // PEER_TOK_BLOCK
    HK = PEER_HEADS * PEER_TOPK
    xb = x.reshape(nblk, PEER_TOK_BLOCK, D)
    ib = idx.reshape(nblk, PEER_TOK_BLOCK, HK)
    gb = g.reshape(nblk, PEER_TOK_BLOCK, HK)

    def block(args):
        xt, it, gt = args
        u = expert_u[it]
        a = jnp.einsum('tkd,td->tk', u, xt, preferred_element_type=F32)
        hact = jax.nn.gelu(a, approximate=False) * gt
        return jnp.einsum('tk,tkd->td', hact.astype(x.dtype), expert_v[it])

    return lax.map(block, (xb, ib, gb)).reshape(B, S, D)


def setup_inputs(seed: int = 0) -> dict:
    key = jax.random.key(seed)
    ks = jax.random.split(key, 24)
    nrm = lambda k, shape, s: jax.random.normal(k, shape, F32) * s
    dinv = D_MODEL ** -0.5
    gate_b = jnp.stack([nrm(ks[5], (DEPTH, ML_HEADS), 0.1),
                        jnp.linspace(3.0, 6.0, ML_HEADS)[None, :] + nrm(ks[6], (DEPTH, ML_HEADS), 0.1)], axis=1)
    return {
        'x': jax.random.normal(ks[0], (BATCH, SEQ, D_MODEL), F32),
        'mem': jax.random.normal(ks[1], (BATCH, MEM_LEN, D_MODEL), F32),
        'w_in': nrm(ks[2], (DEPTH, D_MODEL, IN_PROJ_W), dinv),
        'ml_conv_w': nrm(ks[3], (DEPTH, CONV_W, 2 * ML_QK_W), CONV_W ** -0.5),
        'ml_conv_b': nrm(ks[4], (DEPTH, 2 * ML_QK_W), 0.02),
        'ml_gate_b': gate_b,
        'ml_norm_w': 1.0 + nrm(ks[7], (DEPTH, ML_V_W), 0.02),
        'fox_f_b': jnp.linspace(1.0, 4.0, FOX_HEADS)[None, :] + nrm(ks[8], (DEPTH, FOX_HEADS), 0.1),
        'w_branch_ml': nrm(ks[9], (DEPTH, ML_V_W, D_MODEL), ML_V_W ** -0.5 * DN_BETA),
        'w_branch_fox': nrm(ks[10], (DEPTH, FOX_W, D_MODEL), FOX_W ** -0.5 * DN_BETA),
        'w_out': nrm(ks[11], (DEPTH, D_MODEL, D_MODEL), dinv * DN_BETA),
        'w_xq': nrm(ks[12], (DEPTH, D_MODEL, D_MODEL), dinv),
        'w_xkv': nrm(ks[13], (DEPTH, D_MODEL, 2 * D_MODEL), dinv),
        'w_xo': nrm(ks[14], (DEPTH, D_MODEL, D_MODEL), dinv * DN_BETA),
        'peer_wq': nrm(ks[15], (DEPTH, D_MODEL, PEER_HEADS * PEER_DK), dinv),
        'peer_sub_keys': nrm(ks[16], (DEPTH, PEER_HEADS, 2, N_KEYS, PEER_DK // 2), (PEER_DK // 2) ** -0.5),
        'peer_u': nrm(ks[17], (DEPTH, N_EXPERTS, D_MODEL), dinv),
        'peer_v': nrm(ks[18], (DEPTH, N_EXPERTS, D_MODEL), DN_BETA * PEER_HEADS ** -0.5),
        'ln_w': 1.0 + nrm(ks[19], (DEPTH, 3, D_MODEL), 0.02),
        'ln_b': nrm(ks[20], (DEPTH, 3, D_MODEL), 0.02),
    }


def reference(x, mem, w_in, ml_conv_w, ml_conv_b, ml_gate_b, ml_norm_w, fox_f_b, w_branch_ml, w_branch_fox,
              w_out, w_xq, w_xkv, w_xo, peer_wq, peer_sub_keys, peer_u, peer_v, ln_w, ln_b):
    for l in range(DEPTH):
        h = hybrid_mixer(x, w_in[l], ml_conv_w[l], ml_conv_b[l], ml_gate_b[l], ml_norm_w[l], fox_f_b[l],
                         w_branch_ml[l], w_branch_fox[l], w_out[l])
        x = layer_norm(DN_ALPHA * x + h, ln_w[l, 0], ln_b[l, 0])
        h = memory_cross_attention(x, mem, w_xq[l], w_xkv[l], w_xo[l])
        x = layer_norm(DN_ALPHA * x + h, ln_w[l, 1], ln_b[l, 1])
        h = peer_ffn(x, peer_wq[l], peer_sub_keys[l], peer_u[l], peer_v[l])
        x = layer_norm(DN_ALPHA * x + h, ln_w[l, 2], ln_b[l, 2])
    return x
```

```python
import functools
import math

import jax
import jax.numpy as jnp
from jax import lax
from jax.experimental import pallas as pl
from jax.experimental.pallas import tpu as pltpu

F32 = jnp.float32
BF16 = jnp.bfloat16

ML_HEADS = 4
ML_QK = 256
ML_V = 512
CONV_W = 4
FOX_HEADS = 16
FOX_HD = 128
X_HEADS = 4
PEER_HEADS = 8
N_KEYS = 128
PEER_TOPK = 16
LN_EPS = 1e-5
N_GATE_ROWS = 32

VMEM_LIMIT_BYTES = 56 * 1024 * 1024
LANES = 128

_NT = (((1,), (1,)), ((), ()))
_TN = (((0,), (0,)), ((), ()))


def _params(n_axes):
    return pltpu.CompilerParams(dimension_semantics=("arbitrary",) * n_axes,
                                vmem_limit_bytes=VMEM_LIMIT_BYTES)


def _mm_kernel(x_ref, w_ref, o_ref):
    o_ref[...] = jnp.dot(x_ref[...], w_ref[...], preferred_element_type=F32).astype(o_ref.dtype)


def _mm(x, w, out_dtype, tm, tn, name):
    M, K = x.shape
    N = w.shape[1]
    return pl.pallas_call(
        _mm_kernel, grid=(M // tm, N // tn),
        in_specs=[pl.BlockSpec((tm, K), lambda i, j: (i, 0)),
                  pl.BlockSpec((K, tn), lambda i, j: (0, j))],
        out_specs=pl.BlockSpec((tm, tn), lambda i, j: (i, j)),
        out_shape=jax.ShapeDtypeStruct((M, N), out_dtype),
        compiler_params=_params(2), name=name)(x, w)


def _gates_kernel(x_ref, wg_ref, b_ref, o_ref, carry_ref, *, ts):
    j = pl.program_id(1)

    @pl.when(j == 0)
    def _():
        carry_ref[...] = jnp.zeros_like(carry_ref)

    raw = lax.dot_general(wg_ref[...], x_ref[...], _NT, preferred_element_type=F32)
    pre = raw + b_ref[:, 0:1]
    ls = jnp.minimum(pre, 0.0) - jnp.log1p(jnp.exp(-jnp.abs(pre)))
    lane = lax.broadcasted_iota(jnp.int32, ls.shape, 1)
    c = ls
    sh = 1
    while sh < ts:
        c = c + jnp.where(lane >= sh, pltpu.roll(c, sh, axis=1), 0.0)
        sh *= 2
    c = c + carry_ref[:, 0:1]
    carry_ref[...] = jnp.broadcast_to(c[:, ts - 1:ts], carry_ref.shape)
    row = lax.broadcasted_iota(jnp.int32, ls.shape, 0)
    o_ref[0] = jnp.where(row < ML_HEADS, pre, c)


def _gates(xb, wgt, bias, B, S, ts=512):
    K = xb.shape[1]
    ns = S // ts
    return pl.pallas_call(
        functools.partial(_gates_kernel, ts=ts), grid=(B, ns),
        in_specs=[pl.BlockSpec((ts, K), lambda b, j: (b * ns + j, 0)),
                  pl.BlockSpec((N_GATE_ROWS, K), lambda b, j: (0, 0)),
                  pl.BlockSpec((N_GATE_ROWS, LANES), lambda b, j: (0, 0))],
        out_specs=pl.BlockSpec((1, N_GATE_ROWS, ts), lambda b, j: (b, 0, j)),
        out_shape=jax.ShapeDtypeStruct((B, N_GATE_ROWS, S), F32),
        scratch_shapes=[pltpu.VMEM((N_GATE_ROWS, LANES), F32)],
        compiler_params=_params(2), name="gates")(xb, wgt, bias)


def _conv_kernel(x_ref, w_ref, b_ref, sc_ref, o_ref):
    x = x_ref[...]
    rows = lax.broadcasted_iota(jnp.int32, x.shape, 0)
    y = x * w_ref[CONV_W - 1:CONV_W, :] + b_ref[...]
    for d in range(1, CONV_W):
        xs = jnp.where(rows >= d, pltpu.roll(x, d, axis=0), 0.0)
        y = y + xs * w_ref[CONV_W - 1 - d:CONV_W - d, :]
    y = y * jax.nn.sigmoid(y)
    o_ref[...] = (y * sc_ref[...]).astype(o_ref.dtype)


def _conv_silu(qk_src, conv_w, conv_b, col_scale, B, S, tc=256):
    C = conv_w.shape[1]
    return pl.pallas_call(
        _conv_kernel, grid=(B, C // tc),
        in_specs=[pl.BlockSpec((S, tc), lambda b, j: (b, j)),
                  pl.BlockSpec((CONV_W, tc), lambda b, j: (0, j)),
                  pl.BlockSpec((1, tc), lambda b, j: (0, j)),
                  pl.BlockSpec((1, tc), lambda b, j: (0, j))],
        out_specs=pl.BlockSpec((S, tc), lambda b, j: (b, j)),
        out_shape=jax.ShapeDtypeStruct((B * S, C), BF16),
        compiler_params=_params(2), name="conv_silu")(qk_src, conv_w, conv_b, col_scale)


def _row_to_col(row, eye):
    return jnp.sum(jnp.where(eye, row, 0.0), axis=1, keepdims=True)


def _mlstm_kernel(q_ref, k_ref, v_ref, og_ref, gt_ref, nw_ref, out_ref,
                  C_ref, n_ref, m_ref, g_ref, *, L):
    c = pl.program_id(1)

    @pl.when(c == 0)
    def _():
        C_ref[...] = jnp.zeros_like(C_ref)
        n_ref[...] = jnp.zeros_like(n_ref)
        m_ref[...] = jnp.zeros_like(m_ref)
        g_ref[...] = jnp.zeros_like(g_ref)

    ri = lax.broadcasted_iota(jnp.int32, (L, L), 0)
    ci = lax.broadcasted_iota(jnp.int32, (L, L), 1)
    eye = ri == ci
    causal = ci <= ri
    for h in range(ML_HEADS):
        ig = gt_ref[0, h:h + 1, :]
        gsum = gt_ref[0, ML_HEADS + h:ML_HEADS + h + 1, :]
        g0 = g_ref[h:h + 1, 0:1]
        m_prev = m_ref[h:h + 1, 0:1]
        b_row = gsum - g0
        b_last = b_row[:, L - 1:L]
        u_row = ig - b_row
        b_col = _row_to_col(b_row, eye)
        u_col = _row_to_col(u_row, eye)
        dlog = jnp.where(causal, b_col + u_row, -jnp.inf)
        a_col = b_col + m_prev
        m_t = jnp.maximum(a_col, jnp.max(dlog, axis=1, keepdims=True))
        qh = q_ref[:, h * ML_QK:(h + 1) * ML_QK]
        kh = k_ref[:, h * ML_QK:(h + 1) * ML_QK]
        vh = v_ref[:, h * ML_V:(h + 1) * ML_V]
        s = lax.dot_general(qh, kh, _NT, preferred_element_type=F32)
        p = jnp.exp(dlog - m_t) * s
        inter = jnp.exp(a_col - m_t)
        c_prev = C_ref[h]
        num = (jnp.dot(p.astype(BF16), vh, preferred_element_type=F32)
               + inter * jnp.dot(qh, c_prev.astype(BF16), preferred_element_type=F32))
        n_row = n_ref[h:h + 1, :]
        den = (jnp.sum(p, axis=1, keepdims=True)
               + inter * jnp.sum(qh.astype(F32) * n_row, axis=1, keepdims=True))
        ht = num / jnp.maximum(jnp.abs(den), jnp.exp(-m_t))
        w_end = b_last + u_col
        m_new = jnp.maximum(b_last + m_prev, jnp.max(w_end, axis=0, keepdims=True))
        decay = jnp.exp(b_last + m_prev - m_new)
        wk = jnp.exp(w_end - m_new) * kh.astype(F32)
        C_ref[h] = decay * c_prev + lax.dot_general(wk.astype(BF16), vh, _TN,
                                                    preferred_element_type=F32)
        n_ref[h:h + 1, :] = decay * n_row + jnp.sum(wk, axis=0, keepdims=True)
        m_ref[h:h + 1, :] = jnp.broadcast_to(m_new, (1, LANES))
        g_ref[h:h + 1, :] = jnp.broadcast_to(gsum[:, L - 1:L], (1, LANES))
        mu = jnp.mean(ht, axis=1, keepdims=True)
        d = ht - mu
        var = jnp.mean(d * d, axis=1, keepdims=True)
        hn = d * lax.rsqrt(var + LN_EPS) * nw_ref[:, h * ML_V:(h + 1) * ML_V]
        og = og_ref[:, h * ML_V:(h + 1) * ML_V]
        out_ref[:, h * ML_V:(h + 1) * ML_V] = (hn * jax.nn.sigmoid(og)).astype(out_ref.dtype)


def _mlstm(qk, vb_src, v_col, og_src, og_col, gt, norm_w, B, S, L=128):
    nc = S // L
    qw = ML_HEADS * ML_QK
    vw = ML_HEADS * ML_V
    return pl.pallas_call(
        functools.partial(_mlstm_kernel, L=L), grid=(B, nc),
        in_specs=[pl.BlockSpec((L, qw), lambda b, c: (b * nc + c, 0)),
                  pl.BlockSpec((L, qw), lambda b, c: (b * nc + c, 1)),
                  pl.BlockSpec((L, vw), lambda b, c: (b * nc + c, v_col)),
                  pl.BlockSpec((L, vw), lambda b, c: (b * nc + c, og_col)),
                  pl.BlockSpec((1, N_GATE_ROWS, L), lambda b, c: (b, 0, c)),
                  pl.BlockSpec((1, vw), lambda b, c: (0, 0))],
        out_specs=pl.BlockSpec((L, vw), lambda b, c: (b * nc + c, 0)),
        out_shape=jax.ShapeDtypeStruct((B * S, vw), BF16),
        scratch_shapes=[pltpu.VMEM((ML_HEADS, ML_QK, ML_V), F32),
                        pltpu.VMEM((8, ML_QK), F32),
                        pltpu.VMEM((8, LANES), F32),
                        pltpu.VMEM((8, LANES), F32)],
        compiler_params=_params(2), name="mlstm")(qk, qk, vb_src, og_src, gt, norm_w)


def _fox_kernel(q_ref, k_ref, v_ref, fq_ref, fk_ref, o_ref, m_sc, l_sc, acc_sc, *, tq, tk):
    i = pl.program_id(2)
    scale = FOX_HD ** -0.5
    q = q_ref[...]
    ri = lax.broadcasted_iota(jnp.int32, (tq, tq), 0)
    ci = lax.broadcasted_iota(jnp.int32, (tq, tq), 1)
    fq_col = _row_to_col(fq_ref[0, 0], ri == ci)
    m_sc[...] = jnp.full_like(m_sc, -1e30)
    l_sc[...] = jnp.zeros_like(l_sc)
    acc_sc[...] = jnp.zeros_like(acc_sc)

    def step(j, masked):
        kj = k_ref[pl.ds(pl.multiple_of(j * tk, tk), tk), :]
        vj = v_ref[pl.ds(pl.multiple_of(j * tk, tk), tk), :]
        s = lax.dot_general(q, kj, _NT, preferred_element_type=F32) * scale
        s = s + fq_col - fk_ref[0, 0, j]
        if masked:
            s = jnp.where(ci <= ri, s, -jnp.inf)
        m_old = m_sc[...]
        m_new = jnp.maximum(m_old, jnp.max(s, axis=1, keepdims=True))
        alpha = jnp.exp(m_old - m_new)
        p = jnp.exp(s - m_new)
        l_sc[...] = alpha * l_sc[...] + jnp.sum(p, axis=1, keepdims=True)
        acc_sc[...] = alpha * acc_sc[...] + jnp.dot(p.astype(BF16), vj, preferred_element_type=F32)
        m_sc[...] = m_new

    def body(j, carry):
        step(j, False)
        return carry

    lax.fori_loop(0, i, body, 0)
    step(i, True)
    o_ref[...] = (acc_sc[...] / l_sc[...]).astype(o_ref.dtype)


def _fox(src, q_col, k_col, v_col, gt4, B, S, t=512):
    assert S % t == 0
    nq = S // t
    frow = 2 * ML_HEADS
    gt5 = gt4.reshape(B, N_GATE_ROWS, nq, 1, t)
    return pl.pallas_call(
        functools.partial(_fox_kernel, tq=t, tk=t), grid=(B, FOX_HEADS, nq),
        in_specs=[pl.BlockSpec((t, FOX_HD), lambda b, h, i: (b * nq + i, q_col + h)),
                  pl.BlockSpec((S, FOX_HD), lambda b, h, i: (b, k_col + h)),
                  pl.BlockSpec((S, FOX_HD), lambda b, h, i: (b, v_col + h)),
                  pl.BlockSpec((1, 1, 1, t), lambda b, h, i: (b, frow + h, 0, i)),
                  pl.BlockSpec((1, 1, nq, 1, t), lambda b, h, i: (b, frow + h, 0, 0, 0))],
        out_specs=pl.BlockSpec((t, FOX_HD), lambda b, h, i: (b * nq + i, h)),
        out_shape=jax.ShapeDtypeStruct((B * S, FOX_HEADS * FOX_HD), BF16),
        scratch_shapes=[pltpu.VMEM((t, 1), F32), pltpu.VMEM((t, 1), F32),
                        pltpu.VMEM((t, FOX_HD), F32)],
        compiler_params=_params(3), name="fox_attn")(src, src, src, gt4, gt5)


def _merge_kernel(h_ref, y_ref, wm_ref, wf_ref, gm_ref, gf_ref, o_ref):
    a = jnp.dot(h_ref[...], wm_ref[...], preferred_element_type=F32)
    b = jnp.dot(y_ref[...], wf_ref[...], preferred_element_type=F32)
    o_ref[...] = (jax.nn.sigmoid(gm_ref[...]) * a + jax.nn.sigmoid(gf_ref[...]) * b).astype(o_ref.dtype)


def _merge(h, y, wm, wf, g_src, gm_col, gf_col, tm=512, tn=512):
    M, K = h.shape
    N = wm.shape[1]
    nb = N // tn
    return pl.pallas_call(
        _merge_kernel, grid=(M // tm, nb),
        in_specs=[pl.BlockSpec((tm, K), lambda i, j: (i, 0)),
                  pl.BlockSpec((tm, K), lambda i, j: (i, 0)),
                  pl.BlockSpec((K, tn), lambda i, j: (0, j)),
                  pl.BlockSpec((K, tn), lambda i, j: (0, j)),
                  pl.BlockSpec((tm, tn), lambda i, j: (i, gm_col * nb + j)),
                  pl.BlockSpec((tm, tn), lambda i, j: (i, gf_col * nb + j))],
        out_specs=pl.BlockSpec((tm, tn), lambda i, j: (i, j)),
        out_shape=jax.ShapeDtypeStruct((M, N), BF16),
        compiler_params=_params(2), name="merge")(h, y, wm, wf, g_src, g_src)


def _ln(z, w, b):
    mu = jnp.mean(z, axis=-1, keepdims=True)
    d = z - mu
    var = jnp.mean(d * d, axis=-1, keepdims=True)
    return d * lax.rsqrt(var + LN_EPS) * w + b


def _mm_ln_kernel(a_ref, w_ref, x_ref, lw_ref, lb_ref, o_ref, ob_ref, *, alpha):
    hproj = jnp.dot(a_ref[...], w_ref[...], preferred_element_type=F32)
    out = _ln(alpha * x_ref[...] + hproj, lw_ref[...], lb_ref[...])
    o_ref[...] = out
    ob_ref[...] = out.astype(BF16)


def _mm_ln(a, w, xres, lw, lb, alpha, tm=256):
    M, K = a.shape
    N = w.shape[1]
    return pl.pallas_call(
        functools.partial(_mm_ln_kernel, alpha=alpha), grid=(M // tm,),
        in_specs=[pl.BlockSpec((tm, K), lambda i: (i, 0)),
                  pl.BlockSpec((K, N), lambda i: (0, 0)),
                  pl.BlockSpec((tm, N), lambda i: (i, 0)),
                  pl.BlockSpec((1, N), lambda i: (0, 0)),
                  pl.BlockSpec((1, N), lambda i: (0, 0))],
        out_specs=[pl.BlockSpec((tm, N), lambda i: (i, 0)),
                   pl.BlockSpec((tm, N), lambda i: (i, 0))],
        out_shape=[jax.ShapeDtypeStruct((M, N), F32), jax.ShapeDtypeStruct((M, N), BF16)],
        compiler_params=_params(1), name="proj_ln")(a, w, xres, lw, lb)


def _add_ln_kernel(h_ref, x_ref, lw_ref, lb_ref, o_ref, *, alpha):
    o_ref[...] = _ln(alpha * x_ref[...] + h_ref[...], lw_ref[...], lb_ref[...])


def _add_ln(h, xres, lw, lb, alpha, tm=512):
    M, N = h.shape
    return pl.pallas_call(
        functools.partial(_add_ln_kernel, alpha=alpha), grid=(M // tm,),
        in_specs=[pl.BlockSpec((tm, N), lambda i: (i, 0)),
                  pl.BlockSpec((tm, N), lambda i: (i, 0)),
                  pl.BlockSpec((1, N), lambda i: (0, 0)),
                  pl.BlockSpec((1, N), lambda i: (0, 0))],
        out_specs=pl.BlockSpec((tm, N), lambda i: (i, 0)),
        out_shape=jax.ShapeDtypeStruct((M, N), F32),
        compiler_params=_params(1), name="add_ln")(h, xres, lw, lb)


def _xattn_kernel(q_ref, kv_ref, o_ref, *, hd, d_model):
    scale = hd ** -0.5
    for h in range(X_HEADS):
        qh = q_ref[:, h * hd:(h + 1) * hd]
        kh = kv_ref[:, h * hd:(h + 1) * hd]
        vh = kv_ref[:, d_model + h * hd:d_model + (h + 1) * hd]
        s = lax.dot_general(qh, kh, _NT, preferred_element_type=F32) * scale
        m = jnp.max(s, axis=1, keepdims=True)
        p = jnp.exp(s - m)
        l = jnp.sum(p, axis=1, keepdims=True)
        o = jnp.dot(p.astype(BF16), vh, preferred_element_type=F32) / l
        o_ref[:, h * hd:(h + 1) * hd] = o.astype(o_ref.dtype)


def _xattn(q, kv, B, S, mem_len, tq=512):
    D = q.shape[1]
    nq = S // tq
    return pl.pallas_call(
        functools.partial(_xattn_kernel, hd=D // X_HEADS, d_model=D), grid=(B, nq),
        in_specs=[pl.BlockSpec((tq, D), lambda b, i: (b * nq + i, 0)),
                  pl.BlockSpec((mem_len, 2 * D), lambda b, i: (b, 0))],
        out_specs=pl.BlockSpec((tq, D), lambda b, i: (b * nq + i, 0)),
        out_shape=jax.ShapeDtypeStruct((B * S, D), BF16),
        compiler_params=_params(2), name="xattn")(q, kv)


def _top_values(work, k):
    vals = []
    for _ in range(k):
        m = jnp.max(work, axis=0, keepdims=True)
        vals.append(m)
        work = jnp.where(work == m, -jnp.inf, work)
    return vals


def _stack_rows(rows, n):
    ri = lax.broadcasted_iota(jnp.int32, (n, rows[0].shape[1]), 0)
    out = jnp.zeros((n, rows[0].shape[1]), F32)
    for r, v in enumerate(rows):
        out = jnp.where(ri == r, v, out)
    return out


def _peer_score_kernel(q_ref, keys_ref, s1_ref, s2_ref, st_ref):
    dh = q_ref.shape[1] // (2 * PEER_HEADS)
    for h in range(PEER_HEADS):
        q1 = q_ref[:, (2 * h) * dh:(2 * h + 1) * dh]
        q2 = q_ref[:, (2 * h + 1) * dh:(2 * h + 2) * dh]
        s1 = lax.dot_general(keys_ref[2 * h], q1, _NT, preferred_element_type=F32)
        s2 = lax.dot_general(keys_ref[2 * h + 1], q2, _NT, preferred_element_type=F32)
        s1_ref[h] = s1
        s2_ref[h] = s2
        t1 = _top_values(s1, PEER_TOPK)
        t2 = _stack_rows(_top_values(s2, PEER_TOPK), PEER_TOPK)
        cand = jnp.concatenate([t1[a] + t2 for a in range(PEER_TOPK)], axis=0)
        best = _top_values(cand, PEER_TOPK)
        m = best[0]
        z = jnp.zeros_like(m)
        for r in range(PEER_TOPK):
            z = z + jnp.exp(best[r] - m)
        tau = best[PEER_TOPK - 1]
        mz = m + jnp.log(z)
        st_ref[h] = _stack_rows([tau, mz], 8)


def _peer_scores(pq, keys, tm=256):
    M, W = pq.shape
    return pl.pallas_call(
        _peer_score_kernel, grid=(M // tm,),
        in_specs=[pl.BlockSpec((tm, W), lambda i: (i, 0)),
                  pl.BlockSpec(keys.shape, lambda i: (0, 0, 0))],
        out_specs=[pl.BlockSpec((PEER_HEADS, N_KEYS, tm), lambda i: (0, 0, i)),
                   pl.BlockSpec((PEER_HEADS, N_KEYS, tm), lambda i: (0, 0, i)),
                   pl.BlockSpec((PEER_HEADS, 8, tm), lambda i: (0, 0, i))],
        out_shape=[jax.ShapeDtypeStruct((PEER_HEADS, N_KEYS, M), F32),
                   jax.ShapeDtypeStruct((PEER_HEADS, N_KEYS, M), F32),
                   jax.ShapeDtypeStruct((PEER_HEADS, 8, M), F32)],
        compiler_params=_params(1), name="peer_scores")(pq, keys)


def _gelu(a):
    return 0.5 * a * (1.0 + lax.erf(a * (1.0 / math.sqrt(2.0))))


def _peer_mix_kernel(x_ref, u_ref, vt_ref, s1_ref, s2_ref, st_ref, o_ref, *, ti):
    e = pl.program_id(1)

    @pl.when(e == 0)
    def _():
        o_ref[...] = jnp.zeros_like(o_ref)

    a_t = lax.dot_general(u_ref[...], x_ref[...], _NT, preferred_element_type=F32)
    rows = []
    for il in range(ti):
        i = e * ti + il
        acc = None
        for h in range(PEER_HEADS):
            tot = s1_ref[h, pl.ds(i, 1), :] + s2_ref[h]
            w = jnp.where(tot >= st_ref[h, 0:1, :], jnp.exp(tot - st_ref[h, 1:2, :]), 0.0)
            acc = w if acc is None else acc + w
        rows.append(acc)
    gate = jnp.concatenate(rows, axis=0)
    h_t = (_gelu(a_t) * gate).astype(BF16)
    o_ref[...] += jnp.dot(vt_ref[...], h_t, preferred_element_type=F32)


def _peer_mix(xb, u, vt, s1, s2, st, tm=512, ti=4):
    M, D = xb.shape
    E = u.shape[0]
    te = ti * N_KEYS
    return pl.pallas_call(
        functools.partial(_peer_mix_kernel, ti=ti), grid=(M // tm, E // te),
        in_specs=[pl.BlockSpec((tm, D), lambda i, e: (i, 0)),
                  pl.BlockSpec((te, D), lambda i, e: (e, 0)),
                  pl.BlockSpec((D, te), lambda i, e: (0, e)),
                  pl.BlockSpec((PEER_HEADS, N_KEYS, tm), lambda i, e: (0, 0, i)),
                  pl.BlockSpec((PEER_HEADS, N_KEYS, tm), lambda i, e: (0, 0, i)),
                  pl.BlockSpec((PEER_HEADS, 8, tm), lambda i, e: (0, 0, i))],
        out_specs=pl.BlockSpec((D, tm), lambda i, e: (0, i)),
        out_shape=jax.ShapeDtypeStruct((D, M), F32),
        compiler_params=_params(2), name="peer_mix")(xb, u, vt, s1, s2, st)


def _layer(xf, xb, mem_b, p, B, S, alpha):
    D = xf.shape[1]
    w_in = p["w_in"]
    qw = ML_HEADS * ML_QK
    vw = ML_HEADS * ML_V
    fw = FOX_HEADS * FOX_HD
    o = [0]
    for width in (qw, qw, vw, vw, ML_HEADS, ML_HEADS, fw, fw, fw, FOX_HEADS, D, D):
        o.append(o[-1] + width)
    seg = lambda a, b: w_in[:, o[a]:o[b]].astype(BF16)
    w_a = jnp.concatenate([seg(0, 2), seg(3, 4), seg(10, 12)], axis=1)
    w_b = jnp.concatenate([seg(2, 3), seg(6, 9)], axis=1)
    pa = _mm(xb, w_a, F32, 1024, 512, "in_proj_f32")
    pb = _mm(xb, w_b, BF16, 1024, 512, "in_proj_bf16")
    n_g = 2 * ML_HEADS + FOX_HEADS
    wgt = jnp.concatenate([w_in[:, o[4]:o[6]], w_in[:, o[9]:o[10]]], axis=1).T.astype(BF16)
    wgt = jnp.pad(wgt, ((0, N_GATE_ROWS - n_g), (0, 0)))
    gbias = jnp.concatenate([p["ml_gate_b"][0], p["ml_gate_b"][1], p["fox_f_b"],
                             jnp.zeros((N_GATE_ROWS - n_g,), F32)]).astype(F32)
    gbias = jnp.broadcast_to(gbias[:, None], (N_GATE_ROWS, LANES))
    gt = _gates(xb, wgt, gbias, B, S)

    col_scale = jnp.concatenate([jnp.ones((qw,), F32), jnp.full((qw,), ML_QK ** -0.5, F32)])[None, :]
    qk = _conv_silu(pa, p["ml_conv_w"], p["ml_conv_b"][None, :], col_scale, B, S)
    hg = _mlstm(qk, pb, 0, pa, (2 * qw) // vw, gt, p["ml_norm_w"][None, :], B, S)
    nfb = fw // FOX_HD
    y = _fox(pb, (vw // FOX_HD), (vw // FOX_HD) + nfb, (vw // FOX_HD) + 2 * nfb,
             gt.reshape(B, N_GATE_ROWS, 1, S), B, S)
    g_col0 = (2 * qw + vw) // D
    merged = _merge(hg, y, p["w_branch_ml"].astype(BF16), p["w_branch_fox"].astype(BF16),
                    pa, g_col0, g_col0 + 1)
    x1, x1b = _mm_ln(merged, p["w_out"].astype(BF16), xf, p["ln_w"][0:1], p["ln_b"][0:1], alpha)

    xq = _mm(x1b, p["w_xq"].astype(BF16), BF16, 1024, 512, "xattn_q")
    mem_len = mem_b.shape[0] // B
    kv = _mm(mem_b, p["w_xkv"].astype(BF16), BF16, 512, 512, "xattn_kv")
    xo = _xattn(xq, kv, B, S, mem_len)
    x2, x2b = _mm_ln(xo, p["w_xo"].astype(BF16), x1, p["ln_w"][1:2], p["ln_b"][1:2], alpha)

    pq = _mm(x2b, p["peer_wq"].astype(BF16), BF16, 1024, 512, "peer_q")
    keys = p["peer_sub_keys"].reshape(2 * PEER_HEADS, N_KEYS, -1).astype(BF16)
    s1, s2, st = _peer_scores(pq, keys)
    ht = _peer_mix(x2b, p["peer_u"].astype(BF16), p["peer_v"].T.astype(BF16), s1, s2, st)
    x3 = _add_ln(ht.T, x2, p["ln_w"][2:3], p["ln_b"][2:3], alpha)
    return x3


def kernel(x, mem, w_in, ml_conv_w, ml_conv_b, ml_gate_b, ml_norm_w, fox_f_b, w_branch_ml, w_branch_fox,
           w_out, w_xq, w_xkv, w_xo, peer_wq, peer_sub_keys, peer_u, peer_v, ln_w, ln_b):
    B, S, D = x.shape
    depth = w_in.shape[0]
    alpha = (2 * depth) ** 0.25
    xf = x.reshape(B * S, D)
    mem_b = mem.reshape(-1, D).astype(BF16)
    for l in range(depth):
        p = dict(w_in=w_in[l], ml_conv_w=ml_conv_w[l], ml_conv_b=ml_conv_b[l], ml_gate_b=ml_gate_b[l],
                 ml_norm_w=ml_norm_w[l], fox_f_b=fox_f_b[l], w_branch_ml=w_branch_ml[l],
                 w_branch_fox=w_branch_fox[l], w_out=w_out[l], w_xq=w_xq[l], w_xkv=w_xkv[l], w_xo=w_xo[l],
                 peer_wq=peer_wq[l], peer_sub_keys=peer_sub_keys[l], peer_u=peer_u[l], peer_v=peer_v[l],
                 ln_w=ln_w[l], ln_b=ln_b[l])
        xf = _layer(xf, xf.astype(BF16), mem_b, p, B, S, alpha)
    return xf.reshape(B, S, D)
```

```python
import functools
import math

import jax
import jax.numpy as jnp
from jax import lax
from jax.experimental import pallas as pl
from jax.experimental.pallas import tpu as pltpu

F32 = jnp.float32
BF16 = jnp.bfloat16

ML_HEADS = 4
ML_QK = 256
ML_V = 512
CONV_W = 4
FOX_HEADS = 16
FOX_HD = 128
X_HEADS = 4
PEER_HEADS = 8
N_KEYS = 128
PEER_TOPK = 16
LN_EPS = 1e-5
N_GATE_ROWS = 32

VMEM_LIMIT_BYTES = 56 * 1024 * 1024
LANES = 128

_NT = (((1,), (1,)), ((), ()))
_TN = (((0,), (0,)), ((), ()))


def _params(n_axes):
    return pltpu.CompilerParams(dimension_semantics=("arbitrary",) * n_axes,
                                vmem_limit_bytes=VMEM_LIMIT_BYTES)


def _mm_kernel(x_ref, w_ref, o_ref):
    o_ref[...] = jnp.dot(x_ref[...], w_ref[...], preferred_element_type=F32).astype(o_ref.dtype)


def _mm(x, w, out_dtype, tm, tn, name):
    M, K = x.shape
    N = w.shape[1]
    return pl.pallas_call(
        _mm_kernel, grid=(M // tm, N // tn),
        in_specs=[pl.BlockSpec((tm, K), lambda i, j: (i, 0)),
                  pl.BlockSpec((K, tn), lambda i, j: (0, j))],
        out_specs=pl.BlockSpec((tm, tn), lambda i, j: (i, j)),
        out_shape=jax.ShapeDtypeStruct((M, N), out_dtype),
        compiler_params=_params(2), name=name)(x, w)


def _gates_kernel(x_ref, wg_ref, b_ref, o_ref, carry_ref, *, ts):
    j = pl.program_id(1)

    @pl.when(j == 0)
    def _():
        carry_ref[...] = jnp.zeros_like(carry_ref)

    raw = lax.dot_general(wg_ref[...], x_ref[...], _NT, preferred_element_type=F32)
    pre = raw + b_ref[:, 0:1]
    ls = jnp.minimum(pre, 0.0) - jnp.log1p(jnp.exp(-jnp.abs(pre)))
    lane = lax.broadcasted_iota(jnp.int32, ls.shape, 1)
    c = ls
    sh = 1
    while sh < ts:
        c = c + jnp.where(lane >= sh, pltpu.roll(c, sh, axis=1), 0.0)
        sh *= 2
    c = c + carry_ref[:, 0:1]
    carry_ref[...] = jnp.broadcast_to(c[:, ts - 1:ts], carry_ref.shape)
    row = lax.broadcasted_iota(jnp.int32, ls.shape, 0)
    o_ref[0] = jnp.where(row < ML_HEADS, pre, c)


def _gates(xb, wgt, bias, B, S, ts=512):
    K = xb.shape[1]
    ns = S // ts
    return pl.pallas_call(
        functools.partial(_gates_kernel, ts=ts), grid=(B, ns),
        in_specs=[pl.BlockSpec((ts, K), lambda b, j: (b * ns + j, 0)),
                  pl.BlockSpec((N_GATE_ROWS, K), lambda b, j: (0, 0)),
                  pl.BlockSpec((N_GATE_ROWS, LANES), lambda b, j: (0, 0))],
        out_specs=pl.BlockSpec((1, N_GATE_ROWS, ts), lambda b, j: (b, 0, j)),
        out_shape=jax.ShapeDtypeStruct((B, N_GATE_ROWS, S), F32),
        scratch_shapes=[pltpu.VMEM((N_GATE_ROWS, LANES), F32)],
        compiler_params=_params(2), name="gates")(xb, wgt, bias)


def _conv_kernel(x_ref, w_ref, b_ref, sc_ref, o_ref):
    x = x_ref[...]
    rows = lax.broadcasted_iota(jnp.int32, x.shape, 0)
    y = x * w_ref[CONV_W - 1:CONV_W, :] + b_ref[...]
    for d in range(1, CONV_W):
        xs = jnp.where(rows >= d, pltpu.roll(x, d, axis=0), 0.0)
        y = y + xs * w_ref[CONV_W - 1 - d:CONV_W - d, :]
    y = y * jax.nn.sigmoid(y)
    o_ref[...] = (y * sc_ref[...]).astype(o_ref.dtype)


def _conv_silu(qk_src, conv_w, conv_b, col_scale, B, S, tc=256):
    C = conv_w.shape[1]
    return pl.pallas_call(
        _conv_kernel, grid=(B, C // tc),
        in_specs=[pl.BlockSpec((S, tc), lambda b, j: (b, j)),
                  pl.BlockSpec((CONV_W, tc), lambda b, j: (0, j)),
                  pl.BlockSpec((1, tc), lambda b, j: (0, j)),
                  pl.BlockSpec((1, tc), lambda b, j: (0, j))],
        out_specs=pl.BlockSpec((S, tc), lambda b, j: (b, j)),
        out_shape=jax.ShapeDtypeStruct((B * S, C), BF16),
        compiler_params=_params(2), name="conv_silu")(qk_src, conv_w, conv_b, col_scale)


def _row_to_col(row, eye):
    return jnp.sum(jnp.where(eye, row, 0.0), axis=1, keepdims=True)


def _mlstm_kernel(q_ref, k_ref, v_ref, og_ref, gt_ref, nw_ref, out_ref,
                  C_ref, n_ref, m_ref, g_ref, *, L):
    c = pl.program_id(1)

    @pl.when(c == 0)
    def _():
        C_ref[...] = jnp.zeros_like(C_ref)
        n_ref[...] = jnp.zeros_like(n_ref)
        m_ref[...] = jnp.zeros_like(m_ref)
        g_ref[...] = jnp.zeros_like(g_ref)

    ri = lax.broadcasted_iota(jnp.int32, (L, L), 0)
    ci = lax.broadcasted_iota(jnp.int32, (L, L), 1)
    eye = ri == ci
    causal = ci <= ri
    for h in range(ML_HEADS):
        ig = gt_ref[0, h:h + 1, :]
        gsum = gt_ref[0, ML_HEADS + h:ML_HEADS + h + 1, :]
        g0 = g_ref[h:h + 1, 0:1]
        m_prev = m_ref[h:h + 1, 0:1]
        b_row = gsum - g0
        b_last = b_row[:, L - 1:L]
        u_row = ig - b_row
        b_col = _row_to_col(b_row, eye)
        u_col = _row_to_col(u_row, eye)
        dlog = jnp.where(causal, b_col + u_row, -jnp.inf)
        a_col = b_col + m_prev
        m_t = jnp.maximum(a_col, jnp.max(dlog, axis=1, keepdims=True))
        qh = q_ref[:, h * ML_QK:(h + 1) * ML_QK]
        kh = k_ref[:, h * ML_QK:(h + 1) * ML_QK]
        vh = v_ref[:, h * ML_V:(h + 1) * ML_V]
        s = lax.dot_general(qh, kh, _NT, preferred_element_type=F32)
        p = jnp.exp(dlog - m_t) * s
        inter = jnp.exp(a_col - m_t)
        c_prev = C_ref[h]
        num = (jnp.dot(p.astype(BF16), vh, preferred_element_type=F32)
               + inter * jnp.dot(qh, c_prev.astype(BF16), preferred_element_type=F32))
        n_row = n_ref[h:h + 1, :]
        den = (jnp.sum(p, axis=1, keepdims=True)
               + inter * jnp.sum(qh.astype(F32) * n_row, axis=1, keepdims=True))
        ht = num / jnp.maximum(jnp.abs(den), jnp.exp(-m_t))
        w_end = b_last + u_col
        m_new = jnp.maximum(b_last + m_prev, jnp.max(w_end, axis=0, keepdims=True))
        decay = jnp.exp(b_last + m_prev - m_new)
        wk = jnp.exp(w_end - m_new) * kh.astype(F32)
        C_ref[h] = decay * c_prev + lax.dot_general(wk.astype(BF16), vh, _TN,
                                                    preferred_element_type=F32)
        n_ref[h:h + 1, :] = decay * n_row + jnp.sum(wk, axis=0, keepdims=True)
        m_ref[h:h + 1, :] = jnp.broadcast_to(m_new, (1, LANES))
        g_ref[h:h + 1, :] = jnp.broadcast_to(gsum[:, L - 1:L], (1, LANES))
        mu = jnp.mean(ht, axis=1, keepdims=True)
        d = ht - mu
        var = jnp.mean(d * d, axis=1, keepdims=True)
        hn = d * lax.rsqrt(var + LN_EPS) * nw_ref[:, h * ML_V:(h + 1) * ML_V]
        og = og_ref[:, h * ML_V:(h + 1) * ML_V]
        out_ref[:, h * ML_V:(h + 1) * ML_V] = (hn * jax.nn.sigmoid(og)).astype(out_ref.dtype)


def _mlstm(qk, vb_src, v_col, og_src, og_col, gt, norm_w, B, S, L=128):
    nc = S // L
    qw = ML_HEADS * ML_QK
    vw = ML_HEADS * ML_V
    return pl.pallas_call(
        functools.partial(_mlstm_kernel, L=L), grid=(B, nc),
        in_specs=[pl.BlockSpec((L, qw), lambda b, c: (b * nc + c, 0)),
                  pl.BlockSpec((L, qw), lambda b, c: (b * nc + c, 1)),
                  pl.BlockSpec((L, vw), lambda b, c: (b * nc + c, v_col)),
                  pl.BlockSpec((L, vw), lambda b, c: (b * nc + c, og_col)),
                  pl.BlockSpec((1, N_GATE_ROWS, L), lambda b, c: (b, 0, c)),
                  pl.BlockSpec((1, vw), lambda b, c: (0, 0))],
        out_specs=pl.BlockSpec((L, vw), lambda b, c: (b * nc + c, 0)),
        out_shape=jax.ShapeDtypeStruct((B * S, vw), BF16),
        scratch_shapes=[pltpu.VMEM((ML_HEADS, ML_QK, ML_V), F32),
                        pltpu.VMEM((8, ML_QK), F32),
                        pltpu.VMEM((8, LANES), F32),
                        pltpu.VMEM((8, LANES), F32)],
        compiler_params=_params(2), name="mlstm")(qk, qk, vb_src, og_src, gt, norm_w)


def _fox_kernel(q_ref, k_ref, v_ref, f_ref, o_ref, *, tq):
    S = q_ref.shape[0]
    scale = FOX_HD ** -0.5
    ri = lax.broadcasted_iota(jnp.int32, (tq, tq), 0)
    ci = lax.broadcasted_iota(jnp.int32, (tq, tq), 1)
    causal = ci <= ri
    for i in range(S // tq):
        lo = i * tq
        q = q_ref[lo:lo + tq, :]
        sd = lax.dot_general(q, k_ref[lo:lo + tq, :], _NT, preferred_element_type=F32) * scale
        sd = jnp.where(causal, sd - f_ref[0, 0, :, lo:lo + tq], -jnp.inf)
        m = jnp.max(sd, axis=1, keepdims=True)
        if i > 0:
            sl = lax.dot_general(q, k_ref[0:lo, :], _NT, preferred_element_type=F32) * scale
            sl = sl - f_ref[0, 0, :, 0:lo]
            m = jnp.maximum(m, jnp.max(sl, axis=1, keepdims=True))
        pd = jnp.exp(sd - m)
        l = jnp.sum(pd, axis=1, keepdims=True)
        o = jnp.dot(pd.astype(BF16), v_ref[lo:lo + tq, :], preferred_element_type=F32)
        if i > 0:
            pl_ = jnp.exp(sl - m)
            l = l + jnp.sum(pl_, axis=1, keepdims=True)
            o = o + jnp.dot(pl_.astype(BF16), v_ref[0:lo, :], preferred_element_type=F32)
        o_ref[lo:lo + tq, :] = (o / l).astype(o_ref.dtype)


def _fox(src, q_col, k_col, v_col, gt4, B, S, tq=256):
    frow = 2 * ML_HEADS
    return pl.pallas_call(
        functools.partial(_fox_kernel, tq=tq), grid=(B, FOX_HEADS),
        in_specs=[pl.BlockSpec((S, FOX_HD), lambda b, h: (b, q_col + h)),
                  pl.BlockSpec((S, FOX_HD), lambda b, h: (b, k_col + h)),
                  pl.BlockSpec((S, FOX_HD), lambda b, h: (b, v_col + h)),
                  pl.BlockSpec((1, 1, 1, S), lambda b, h: (b, frow + h, 0, 0))],
        out_specs=pl.BlockSpec((S, FOX_HD), lambda b, h: (b, h)),
        out_shape=jax.ShapeDtypeStruct((B * S, FOX_HEADS * FOX_HD), BF16),
        compiler_params=_params(2), name="fox_attn")(src, src, src, gt4)


def _merge_kernel(h_ref, y_ref, wm_ref, wf_ref, gm_ref, gf_ref, o_ref):
    a = jnp.dot(h_ref[...], wm_ref[...], preferred_element_type=F32)
    b = jnp.dot(y_ref[...], wf_ref[...], preferred_element_type=F32)
    o_ref[...] = (jax.nn.sigmoid(gm_ref[...]) * a + jax.nn.sigmoid(gf_ref[...]) * b).astype(o_ref.dtype)


def _merge(h, y, wm, wf, g_src, gm_col, gf_col, tm=512, tn=512):
    M, K = h.shape
    N = wm.shape[1]
    nb = N // tn
    return pl.pallas_call(
        _merge_kernel, grid=(M // tm, nb),
        in_specs=[pl.BlockSpec((tm, K), lambda i, j: (i, 0)),
                  pl.BlockSpec((tm, K), lambda i, j: (i, 0)),
                  pl.BlockSpec((K, tn), lambda i, j: (0, j)),
                  pl.BlockSpec((K, tn), lambda i, j: (0, j)),
                  pl.BlockSpec((tm, tn), lambda i, j: (i, gm_col * nb + j)),
                  pl.BlockSpec((tm, tn), lambda i, j: (i, gf_col * nb + j))],
        out_specs=pl.BlockSpec((tm, tn), lambda i, j: (i, j)),
        out_shape=jax.ShapeDtypeStruct((M, N), BF16),
        compiler_params=_params(2), name="merge")(h, y, wm, wf, g_src, g_src)


def _ln(z, w, b):
    mu = jnp.mean(z, axis=-1, keepdims=True)
    d = z - mu
    var = jnp.mean(d * d, axis=-1, keepdims=True)
    return d * lax.rsqrt(var + LN_EPS) * w + b


def _mm_ln_kernel(a_ref, w_ref, x_ref, lw_ref, lb_ref, o_ref, ob_ref, *, alpha):
    hproj = jnp.dot(a_ref[...], w_ref[...], preferred_element_type=F32)
    out = _ln(alpha * x_ref[...] + hproj, lw_ref[...], lb_ref[...])
    o_ref[...] = out
    ob_ref[...] = out.astype(BF16)


def _mm_ln(a, w, xres, lw, lb, alpha, tm=256):
    M, K = a.shape
    N = w.shape[1]
    return pl.pallas_call(
        functools.partial(_mm_ln_kernel, alpha=alpha), grid=(M // tm,),
        in_specs=[pl.BlockSpec((tm, K), lambda i: (i, 0)),
                  pl.BlockSpec((K, N), lambda i: (0, 0)),
                  pl.BlockSpec((tm, N), lambda i: (i, 0)),
                  pl.BlockSpec((1, N), lambda i: (0, 0)),
                  pl.BlockSpec((1, N), lambda i: (0, 0))],
        out_specs=[pl.BlockSpec((tm, N), lambda i: (i, 0)),
                   pl.BlockSpec((tm, N), lambda i: (i, 0))],
        out_shape=[jax.ShapeDtypeStruct((M, N), F32), jax.ShapeDtypeStruct((M, N), BF16)],
        compiler_params=_params(1), name="proj_ln")(a, w, xres, lw, lb)


def _add_ln_kernel(ht_ref, x_ref, lw_ref, lb_ref, o_ref, *, alpha):
    o_ref[...] = _ln(alpha * x_ref[...] + ht_ref[...].T, lw_ref[...], lb_ref[...])


def _add_ln(ht, xres, lw, lb, alpha, tm=512):
    N, M = ht.shape
    return pl.pallas_call(
        functools.partial(_add_ln_kernel, alpha=alpha), grid=(M // tm,),
        in_specs=[pl.BlockSpec((N, tm), lambda i: (0, i)),
                  pl.BlockSpec((tm, N), lambda i: (i, 0)),
                  pl.BlockSpec((1, N), lambda i: (0, 0)),
                  pl.BlockSpec((1, N), lambda i: (0, 0))],
        out_specs=pl.BlockSpec((tm, N), lambda i: (i, 0)),
        out_shape=jax.ShapeDtypeStruct((M, N), F32),
        compiler_params=_params(1), name="add_ln")(ht, xres, lw, lb)


def _xattn_kernel(q_ref, kv_ref, o_ref, *, hd, d_model):
    scale = hd ** -0.5
    for h in range(X_HEADS):
        qh = q_ref[:, h * hd:(h + 1) * hd]
        kh = kv_ref[:, h * hd:(h + 1) * hd]
        vh = kv_ref[:, d_model + h * hd:d_model + (h + 1) * hd]
        s = lax.dot_general(qh, kh, _NT, preferred_element_type=F32) * scale
        m = jnp.max(s, axis=1, keepdims=True)
        p = jnp.exp(s - m)
        l = jnp.sum(p, axis=1, keepdims=True)
        o = jnp.dot(p.astype(BF16), vh, preferred_element_type=F32) / l
        o_ref[:, h * hd:(h + 1) * hd] = o.astype(o_ref.dtype)


def _xattn(q, kv, B, S, mem_len, tq=512):
    D = q.shape[1]
    nq = S // tq
    return pl.pallas_call(
        functools.partial(_xattn_kernel, hd=D // X_HEADS, d_model=D), grid=(B, nq),
        in_specs=[pl.BlockSpec((tq, D), lambda b, i: (b * nq + i, 0)),
                  pl.BlockSpec((mem_len, 2 * D), lambda b, i: (b, 0))],
        out_specs=pl.BlockSpec((tq, D), lambda b, i: (b * nq + i, 0)),
        out_shape=jax.ShapeDtypeStruct((B * S, D), BF16),
        compiler_params=_params(2), name="xattn")(q, kv)


def _top_values(work, k):
    vals = []
    for _ in range(k):
        m = jnp.max(work, axis=0, keepdims=True)
        vals.append(m)
        work = jnp.where(work == m, -jnp.inf, work)
    return vals


def _stack_rows(rows, n):
    ri = lax.broadcasted_iota(jnp.int32, (n, rows[0].shape[1]), 0)
    out = jnp.zeros((n, rows[0].shape[1]), F32)
    for r, v in enumerate(rows):
        out = jnp.where(ri == r, v, out)
    return out


def _peer_score_kernel(q_ref, keys_ref, s1_ref, s2_ref, st_ref):
    dh = q_ref.shape[1] // (2 * PEER_HEADS)
    log2e = 1.0 / math.log(2.0)
    for h in range(PEER_HEADS):
        q1 = q_ref[:, (2 * h) * dh:(2 * h + 1) * dh]
        q2 = q_ref[:, (2 * h + 1) * dh:(2 * h + 2) * dh]
        s1 = lax.dot_general(keys_ref[2 * h], q1, _NT, preferred_element_type=F32) * log2e
        s2 = lax.dot_general(keys_ref[2 * h + 1], q2, _NT, preferred_element_type=F32) * log2e
        s1_ref[h] = s1
        s2_ref[h] = s2
        t1 = _top_values(s1, PEER_TOPK)
        t2 = _stack_rows(_top_values(s2, PEER_TOPK), PEER_TOPK)
        cand = jnp.concatenate([t1[a] + t2 for a in range(PEER_TOPK)], axis=0)
        best = _top_values(cand, PEER_TOPK)
        m = best[0]
        z = jnp.zeros_like(m)
        for r in range(PEER_TOPK):
            z = z + jnp.exp2(best[r] - m)
        tau = best[PEER_TOPK - 1]
        mz = m + jnp.log2(z) + 1.0
        st_ref[h, 0:8, :] = jnp.broadcast_to(tau, (8, tau.shape[1]))
        st_ref[h, 8:16, :] = jnp.broadcast_to(mz, (8, mz.shape[1]))


def _peer_scores(pq, keys, tm=256):
    M, W = pq.shape
    return pl.pallas_call(
        _peer_score_kernel, grid=(M // tm,),
        in_specs=[pl.BlockSpec((tm, W), lambda i: (i, 0)),
                  pl.BlockSpec(keys.shape, lambda i: (0, 0, 0))],
        out_specs=[pl.BlockSpec((PEER_HEADS, N_KEYS, tm), lambda i: (0, 0, i)),
                   pl.BlockSpec((PEER_HEADS, N_KEYS, tm), lambda i: (0, 0, i)),
                   pl.BlockSpec((PEER_HEADS, 16, tm), lambda i: (0, 0, i))],
        out_shape=[jax.ShapeDtypeStruct((PEER_HEADS, N_KEYS, M), F32),
                   jax.ShapeDtypeStruct((PEER_HEADS, N_KEYS, M), F32),
                   jax.ShapeDtypeStruct((PEER_HEADS, 16, M), F32)],
        compiler_params=_params(1), name="peer_scores")(pq, keys)


def _peer_mix_kernel(xt_ref, u_ref, vt_ref, s1_ref, s2_ref, st_ref, o_ref, a_sc, h_sc, *, ti, n_tiles, tn):
    g = pl.program_id(1)
    te = ti * N_KEYS

    @pl.when(g == 0)
    def _():
        o_ref[...] = jnp.zeros_like(o_ref)
        a_sc[...] = jnp.zeros_like(a_sc)
        h_sc[...] = jnp.zeros_like(h_sc)

    def stage_a(half, tok):
        return jnp.dot(u_ref[half * te:(half + 1) * te, :], xt_ref[:, tok],
                       preferred_element_type=F32)

    def rep(rows8):
        return jnp.concatenate([rows8] * (N_KEYS // 8), axis=0)

    def stage_b(a_t, tile, tok):
        tile = jnp.clip(tile, 0, n_tiles - 1)
        rows = []
        for il in range(ti):
            i = tile * ti + il
            acc = None
            for h in range(PEER_HEADS):
                tot = s1_ref[h, pl.ds(i, 1), tok] + s2_ref[h, :, tok]
                w = jnp.where(tot >= rep(st_ref[h, 0:8, tok]), jnp.exp2(tot - rep(st_ref[h, 8:16, tok])), 0.0)
                acc = w if acc is None else acc + w
            rows.append(acc)
        half_gate = jnp.concatenate(rows, axis=0)
        erf_term = 1.0 + lax.erf(a_t * (1.0 / math.sqrt(2.0)))
        return (a_t * erf_term * half_gate).astype(BF16)

    for tb in range(xt_ref.shape[1] // tn):
        tok = slice(tb * tn, (tb + 1) * tn)
        a_even = stage_a(0, tok)
        h_odd = stage_b(a_sc[:, tok], 2 * g - 1, tok)
        h_pair = jnp.concatenate([h_sc[:, tok], h_odd], axis=0)
        o_ref[:, tok] += jnp.dot(vt_ref[...], h_pair, preferred_element_type=F32)
        a_sc[:, tok] = stage_a(1, tok)
        h_sc[:, tok] = stage_b(a_even, 2 * g, tok)


def _peer_mix(xbt, u, vt, s1, s2, st, tm=512, ti=4, tn=256):
    D, M = xbt.shape
    E = u.shape[0]
    te = ti * N_KEYS
    n_tiles = E // te
    n_pairs = n_tiles // 2
    return pl.pallas_call(
        functools.partial(_peer_mix_kernel, ti=ti, n_tiles=n_tiles, tn=tn), grid=(M // tm, n_pairs + 1),
        in_specs=[pl.BlockSpec((D, tm), lambda i, g: (0, i)),
                  pl.BlockSpec((2 * te, D), lambda i, g: (jnp.minimum(g, n_pairs - 1), 0)),
                  pl.BlockSpec((D, 2 * te), lambda i, g: (0, jnp.maximum(g - 1, 0))),
                  pl.BlockSpec((PEER_HEADS, N_KEYS, tm), lambda i, g: (0, 0, i)),
                  pl.BlockSpec((PEER_HEADS, N_KEYS, tm), lambda i, g: (0, 0, i)),
                  pl.BlockSpec((PEER_HEADS, 16, tm), lambda i, g: (0, 0, i))],
        out_specs=pl.BlockSpec((D, tm), lambda i, g: (0, i)),
        out_shape=jax.ShapeDtypeStruct((D, M), F32),
        scratch_shapes=[pltpu.VMEM((te, tm), F32), pltpu.VMEM((te, tm), BF16)],
        compiler_params=_params(2), name="peer_mix")(xbt, u, vt, s1, s2, st)


def _layer(xf, xb, mem_b, p, B, S, alpha):
    D = xf.shape[1]
    w_in = p["w_in"]
    qw = ML_HEADS * ML_QK
    vw = ML_HEADS * ML_V
    fw = FOX_HEADS * FOX_HD
    o = [0]
    for width in (qw, qw, vw, vw, ML_HEADS, ML_HEADS, fw, fw, fw, FOX_HEADS, D, D):
        o.append(o[-1] + width)
    seg = lambda a, b: w_in[:, o[a]:o[b]].astype(BF16)
    w_a = jnp.concatenate([seg(0, 2), seg(3, 4), seg(10, 12)], axis=1)
    w_b = jnp.concatenate([seg(2, 3), seg(6, 9)], axis=1)
    pa = _mm(xb, w_a, F32, 1024, 512, "in_proj_f32")
    pb = _mm(xb, w_b, BF16, 1024, 512, "in_proj_bf16")
    n_g = 2 * ML_HEADS + FOX_HEADS
    wgt = jnp.concatenate([w_in[:, o[4]:o[6]], w_in[:, o[9]:o[10]]], axis=1).T.astype(BF16)
    wgt = jnp.pad(wgt, ((0, N_GATE_ROWS - n_g), (0, 0)))
    gbias = jnp.concatenate([p["ml_gate_b"][0], p["ml_gate_b"][1], p["fox_f_b"],
                             jnp.zeros((N_GATE_ROWS - n_g,), F32)]).astype(F32)
    gbias = jnp.broadcast_to(gbias[:, None], (N_GATE_ROWS, LANES))
    gt = _gates(xb, wgt, gbias, B, S)

    col_scale = jnp.concatenate([jnp.ones((qw,), F32), jnp.full((qw,), ML_QK ** -0.5, F32)])[None, :]
    qk = _conv_silu(pa, p["ml_conv_w"], p["ml_conv_b"][None, :], col_scale, B, S)
    hg = _mlstm(qk, pb, 0, pa, (2 * qw) // vw, gt, p["ml_norm_w"][None, :], B, S)
    nfb = fw // FOX_HD
    y = _fox(pb, (vw // FOX_HD), (vw // FOX_HD) + nfb, (vw // FOX_HD) + 2 * nfb,
             gt.reshape(B, N_GATE_ROWS, 1, S), B, S)
    g_col0 = (2 * qw + vw) // D
    merged = _merge(hg, y, p["w_branch_ml"].astype(BF16), p["w_branch_fox"].astype(BF16),
                    pa, g_col0, g_col0 + 1)
    x1, x1b = _mm_ln(merged, p["w_out"].astype(BF16), xf, p["ln_w"][0:1], p["ln_b"][0:1], alpha)

    xq = _mm(x1b, p["w_xq"].astype(BF16), BF16, 1024, 512, "xattn_q")
    mem_len = mem_b.shape[0] // B
    kv = _mm(mem_b, p["w_xkv"].astype(BF16), BF16, 512, 512, "xattn_kv")
    xo = _xattn(xq, kv, B, S, mem_len)
    x2, x2b = _mm_ln(xo, p["w_xo"].astype(BF16), x1, p["ln_w"][1:2], p["ln_b"][1:2], alpha)

    pq = _mm(x2b, p["peer_wq"].astype(BF16), BF16, 1024, 512, "peer_q")
    keys = p["peer_sub_keys"].reshape(2 * PEER_HEADS, N_KEYS, -1).astype(BF16)
    s1, s2, st = _peer_scores(pq, keys)
    ht = _peer_mix(x2b.T, p["peer_u"].astype(BF16), p["peer_v"].T.astype(BF16), s1, s2, st)
    x3 = _add_ln(ht, x2, p["ln_w"][2:3], p["ln_b"][2:3], alpha)
    return x3


def kernel(x, mem, w_in, ml_conv_w, ml_conv_b, ml_gate_b, ml_norm_w, fox_f_b, w_branch_ml, w_branch_fox,
           w_out, w_xq, w_xkv, w_xo, peer_wq, peer_sub_keys, peer_u, peer_v, ln_w, ln_b):
    B, S, D = x.shape
    depth = w_in.shape[0]
    alpha = (2 * depth) ** 0.25
    xf = x.reshape(B * S, D)
    mem_b = mem.reshape(-1, D).astype(BF16)
    for l in range(depth):
        p = dict(w_in=w_in[l], ml_conv_w=ml_conv_w[l], ml_conv_b=ml_conv_b[l], ml_gate_b=ml_gate_b[l],
                 ml_norm_w=ml_norm_w[l], fox_f_b=fox_f_b[l], w_branch_ml=w_branch_ml[l],
                 w_branch_fox=w_branch_fox[l], w_out=w_out[l], w_xq=w_xq[l], w_xkv=w_xkv[l], w_xo=w_xo[l],
                 peer_wq=peer_wq[l], peer_sub_keys=peer_sub_keys[l], peer_u=peer_u[l], peer_v=peer_v[l],
                 ln_w=ln_w[l], ln_b=ln_b[l])
        xf = _layer(xf, xf.astype(BF16), mem_b, p, B, S, alpha)
    return xf.reshape(B, S, D)
```

```python
import functools
import math

import jax
import jax.numpy as jnp
from jax import lax
from jax.experimental import pallas as pl
from jax.experimental.pallas import tpu as pltpu

F32 = jnp.float32
BF16 = jnp.bfloat16

ML_HEADS = 4
ML_QK = 256
ML_V = 512
CONV_W = 4
FOX_HEADS = 16
FOX_HD = 128
X_HEADS = 4
PEER_HEADS = 8
N_KEYS = 128
PEER_TOPK = 16
LN_EPS = 1e-5
N_GATE_ROWS = 32

VMEM_LIMIT_BYTES = 56 * 1024 * 1024
LANES = 128

_NT = (((1,), (1,)), ((), ()))
_TN = (((0,), (0,)), ((), ()))


def _params(n_axes):
    return pltpu.CompilerParams(dimension_semantics=("arbitrary",) * n_axes,
                                vmem_limit_bytes=VMEM_LIMIT_BYTES)


def _mm_kernel(x_ref, w_ref, o_ref):
    o_ref[...] = jnp.dot(x_ref[...], w_ref[...], preferred_element_type=F32).astype(o_ref.dtype)


def _mm(x, w, out_dtype, tm, tn, name):
    M, K = x.shape
    N = w.shape[1]
    return pl.pallas_call(
        _mm_kernel, grid=(M // tm, N // tn),
        in_specs=[pl.BlockSpec((tm, K), lambda i, j: (i, 0)),
                  pl.BlockSpec((K, tn), lambda i, j: (0, j))],
        out_specs=pl.BlockSpec((tm, tn), lambda i, j: (i, j)),
        out_shape=jax.ShapeDtypeStruct((M, N), out_dtype),
        compiler_params=_params(2), name=name)(x, w)


def _transpose_cast_kernel(x_ref, o_ref):
    o_ref[...] = x_ref[...].T.astype(o_ref.dtype)


def _transpose_cast(x, dtype, tr=512):
    R, C = x.shape
    return pl.pallas_call(
        _transpose_cast_kernel, grid=(R // tr,),
        in_specs=[pl.BlockSpec((tr, C), lambda i: (i, 0))],
        out_specs=pl.BlockSpec((C, tr), lambda i: (0, i)),
        out_shape=jax.ShapeDtypeStruct((C, R), dtype),
        compiler_params=_params(1), name="transpose_cast")(x)


def _gates_kernel(x_ref, wg_ref, b_ref, o_ref, carry_ref, *, ts):
    j = pl.program_id(1)

    @pl.when(j == 0)
    def _():
        carry_ref[...] = jnp.zeros_like(carry_ref)

    raw = lax.dot_general(wg_ref[...], x_ref[...], _NT, preferred_element_type=F32)
    pre = raw + b_ref[:, 0:1]
    ls = jnp.minimum(pre, 0.0) - jnp.log1p(jnp.exp(-jnp.abs(pre)))
    lane = lax.broadcasted_iota(jnp.int32, ls.shape, 1)
    c = ls
    sh = 1
    while sh < ts:
        c = c + jnp.where(lane >= sh, pltpu.roll(c, sh, axis=1), 0.0)
        sh *= 2
    c = c + carry_ref[:, 0:1]
    carry_ref[...] = jnp.broadcast_to(c[:, ts - 1:ts], carry_ref.shape)
    row = lax.broadcasted_iota(jnp.int32, ls.shape, 0)
    o_ref[0] = jnp.where(row < ML_HEADS, pre, c)


def _gates(xb, wgt, bias, B, S, ts=512):
    K = xb.shape[1]
    ns = S // ts
    return pl.pallas_call(
        functools.partial(_gates_kernel, ts=ts), grid=(B, ns),
        in_specs=[pl.BlockSpec((ts, K), lambda b, j: (b * ns + j, 0)),
                  pl.BlockSpec((N_GATE_ROWS, K), lambda b, j: (0, 0)),
                  pl.BlockSpec((N_GATE_ROWS, LANES), lambda b, j: (0, 0))],
        out_specs=pl.BlockSpec((1, N_GATE_ROWS, ts), lambda b, j: (b, 0, j)),
        out_shape=jax.ShapeDtypeStruct((B, N_GATE_ROWS, S), F32),
        scratch_shapes=[pltpu.VMEM((N_GATE_ROWS, LANES), F32)],
        compiler_params=_params(2), name="gates")(xb, wgt, bias)


def _conv_kernel(x_ref, w_ref, b_ref, sc_ref, o_ref):
    x = x_ref[...]
    rows = lax.broadcasted_iota(jnp.int32, x.shape, 0)
    y = x * w_ref[CONV_W - 1:CONV_W, :] + b_ref[...]
    for d in range(1, CONV_W):
        xs = jnp.where(rows >= d, pltpu.roll(x, d, axis=0), 0.0)
        y = y + xs * w_ref[CONV_W - 1 - d:CONV_W - d, :]
    y = y * jax.nn.sigmoid(y)
    o_ref[...] = (y * sc_ref[...]).astype(o_ref.dtype)


def _conv_silu(qk_src, conv_w, conv_b, col_scale, B, S, tc=256):
    C = conv_w.shape[1]
    return pl.pallas_call(
        _conv_kernel, grid=(B, C // tc),
        in_specs=[pl.BlockSpec((S, tc), lambda b, j: (b, j)),
                  pl.BlockSpec((CONV_W, tc), lambda b, j: (0, j)),
                  pl.BlockSpec((1, tc), lambda b, j: (0, j)),
                  pl.BlockSpec((1, tc), lambda b, j: (0, j))],
        out_specs=pl.BlockSpec((S, tc), lambda b, j: (b, j)),
        out_shape=jax.ShapeDtypeStruct((B * S, C), BF16),
        compiler_params=_params(2), name="conv_silu")(qk_src, conv_w, conv_b, col_scale)


def _row_to_col(row, eye):
    return jnp.sum(jnp.where(eye, row, 0.0), axis=1, keepdims=True)


def _mlstm_kernel(q_ref, k_ref, v_ref, og_ref, gt_ref, nw_ref, out_ref,
                  C_ref, n_ref, m_ref, g_ref, *, L):
    c = pl.program_id(1)

    @pl.when(c == 0)
    def _():
        C_ref[...] = jnp.zeros_like(C_ref)
        n_ref[...] = jnp.zeros_like(n_ref)
        m_ref[...] = jnp.zeros_like(m_ref)
        g_ref[...] = jnp.zeros_like(g_ref)

    ri = lax.broadcasted_iota(jnp.int32, (L, L), 0)
    ci = lax.broadcasted_iota(jnp.int32, (L, L), 1)
    eye = ri == ci
    causal = ci <= ri
    for h in range(ML_HEADS):
        ig = gt_ref[0, h:h + 1, :]
        gsum = gt_ref[0, ML_HEADS + h:ML_HEADS + h + 1, :]
        g0 = g_ref[h:h + 1, 0:1]
        m_prev = m_ref[h:h + 1, 0:1]
        b_row = gsum - g0
        b_last = b_row[:, L - 1:L]
        u_row = ig - b_row
        b_col = _row_to_col(b_row, eye)
        u_col = _row_to_col(u_row, eye)
        dlog = jnp.where(causal, b_col + u_row, -jnp.inf)
        a_col = b_col + m_prev
        m_t = jnp.maximum(a_col, jnp.max(dlog, axis=1, keepdims=True))
        qh = q_ref[:, h * ML_QK:(h + 1) * ML_QK]
        kh = k_ref[:, h * ML_QK:(h + 1) * ML_QK]
        vh = v_ref[:, h * ML_V:(h + 1) * ML_V]
        s = lax.dot_general(qh, kh, _NT, preferred_element_type=F32)
        p = jnp.exp(dlog - m_t) * s
        inter = jnp.exp(a_col - m_t)
        c_prev = C_ref[h]
        num = (jnp.dot(p.astype(BF16), vh, preferred_element_type=F32)
               + inter * jnp.dot(qh, c_prev.astype(BF16), preferred_element_type=F32))
        n_row = n_ref[h:h + 1, :]
        den = (jnp.sum(p, axis=1, keepdims=True)
               + inter * jnp.sum(qh.astype(F32) * n_row, axis=1, keepdims=True))
        ht = num / jnp.maximum(jnp.abs(den), jnp.exp(-m_t))
        w_end = b_last + u_col
        m_new = jnp.maximum(b_last + m_prev, jnp.max(w_end, axis=0, keepdims=True))
        decay = jnp.exp(b_last + m_prev - m_new)
        wk = jnp.exp(w_end - m_new) * kh.astype(F32)
        C_ref[h] = decay * c_prev + lax.dot_general(wk.astype(BF16), vh, _TN,
                                                    preferred_element_type=F32)
        n_ref[h:h + 1, :] = decay * n_row + jnp.sum(wk, axis=0, keepdims=True)
        m_ref[h:h + 1, :] = jnp.broadcast_to(m_new, (1, LANES))
        g_ref[h:h + 1, :] = jnp.broadcast_to(gsum[:, L - 1:L], (1, LANES))
        mu = jnp.mean(ht, axis=1, keepdims=True)
        d = ht - mu
        var = jnp.mean(d * d, axis=1, keepdims=True)
        hn = d * lax.rsqrt(var + LN_EPS) * nw_ref[:, h * ML_V:(h + 1) * ML_V]
        og = og_ref[:, h * ML_V:(h + 1) * ML_V]
        out_ref[:, h * ML_V:(h + 1) * ML_V] = (hn * jax.nn.sigmoid(og)).astype(out_ref.dtype)


def _mlstm(qk, vb_src, v_col, og_src, og_col, gt, norm_w, B, S, L=128):
    nc = S // L
    qw = ML_HEADS * ML_QK
    vw = ML_HEADS * ML_V
    return pl.pallas_call(
        functools.partial(_mlstm_kernel, L=L), grid=(B, nc),
        in_specs=[pl.BlockSpec((L, qw), lambda b, c: (b * nc + c, 0)),
                  pl.BlockSpec((L, qw), lambda b, c: (b * nc + c, 1)),
                  pl.BlockSpec((L, vw), lambda b, c: (b * nc + c, v_col)),
                  pl.BlockSpec((L, vw), lambda b, c: (b * nc + c, og_col)),
                  pl.BlockSpec((1, N_GATE_ROWS, L), lambda b, c: (b, 0, c)),
                  pl.BlockSpec((1, vw), lambda b, c: (0, 0))],
        out_specs=pl.BlockSpec((L, vw), lambda b, c: (b * nc + c, 0)),
        out_shape=jax.ShapeDtypeStruct((B * S, vw), BF16),
        scratch_shapes=[pltpu.VMEM((ML_HEADS, ML_QK, ML_V), F32),
                        pltpu.VMEM((8, ML_QK), F32),
                        pltpu.VMEM((8, LANES), F32),
                        pltpu.VMEM((8, LANES), F32)],
        compiler_params=_params(2), name="mlstm")(qk, qk, vb_src, og_src, gt, norm_w)


def _fox_kernel(q_ref, k_ref, v_ref, f_ref, o_ref, *, tq):
    S = q_ref.shape[0]
    scale = FOX_HD ** -0.5
    ri = lax.broadcasted_iota(jnp.int32, (tq, tq), 0)
    ci = lax.broadcasted_iota(jnp.int32, (tq, tq), 1)
    causal = ci <= ri
    for i in range(S // tq):
        lo = i * tq
        q = q_ref[lo:lo + tq, :]
        sd = lax.dot_general(q, k_ref[lo:lo + tq, :], _NT, preferred_element_type=F32) * scale
        sd = jnp.where(causal, sd - f_ref[0, 0, :, lo:lo + tq], -jnp.inf)
        m = jnp.max(sd, axis=1, keepdims=True)
        if i > 0:
            sl = lax.dot_general(q, k_ref[0:lo, :], _NT, preferred_element_type=F32) * scale
            sl = sl - f_ref[0, 0, :, 0:lo]
            m = jnp.maximum(m, jnp.max(sl, axis=1, keepdims=True))
        pd = jnp.exp(sd - m)
        l = jnp.sum(pd, axis=1, keepdims=True)
        o = jnp.dot(pd.astype(BF16), v_ref[lo:lo + tq, :], preferred_element_type=F32)
        if i > 0:
            pl_ = jnp.exp(sl - m)
            l = l + jnp.sum(pl_, axis=1, keepdims=True)
            o = o + jnp.dot(pl_.astype(BF16), v_ref[0:lo, :], preferred_element_type=F32)
        o_ref[lo:lo + tq, :] = (o / l).astype(o_ref.dtype)


def _fox(src, q_col, k_col, v_col, gt4, B, S, tq=256):
    frow = 2 * ML_HEADS
    return pl.pallas_call(
        functools.partial(_fox_kernel, tq=tq), grid=(B, FOX_HEADS),
        in_specs=[pl.BlockSpec((S, FOX_HD), lambda b, h: (b, q_col + h)),
                  pl.BlockSpec((S, FOX_HD), lambda b, h: (b, k_col + h)),
                  pl.BlockSpec((S, FOX_HD), lambda b, h: (b, v_col + h)),
                  pl.BlockSpec((1, 1, 1, S), lambda b, h: (b, frow + h, 0, 0))],
        out_specs=pl.BlockSpec((S, FOX_HD), lambda b, h: (b, h)),
        out_shape=jax.ShapeDtypeStruct((B * S, FOX_HEADS * FOX_HD), BF16),
        compiler_params=_params(2), name="fox_attn")(src, src, src, gt4)


def _merge_kernel(h_ref, y_ref, wm_ref, wf_ref, gm_ref, gf_ref, o_ref):
    a = jnp.dot(h_ref[...], wm_ref[...], preferred_element_type=F32)
    b = jnp.dot(y_ref[...], wf_ref[...], preferred_element_type=F32)
    o_ref[...] = (jax.nn.sigmoid(gm_ref[...]) * a + jax.nn.sigmoid(gf_ref[...]) * b).astype(o_ref.dtype)


def _merge(h, y, wm, wf, g_src, gm_col, gf_col, tm=1024, tn=512):
    M, K = h.shape
    N = wm.shape[1]
    nb = N // tn
    return pl.pallas_call(
        _merge_kernel, grid=(M // tm, nb),
        in_specs=[pl.BlockSpec((tm, K), lambda i, j: (i, 0)),
                  pl.BlockSpec((tm, K), lambda i, j: (i, 0)),
                  pl.BlockSpec((K, tn), lambda i, j: (0, j)),
                  pl.BlockSpec((K, tn), lambda i, j: (0, j)),
                  pl.BlockSpec((tm, tn), lambda i, j: (i, gm_col * nb + j)),
                  pl.BlockSpec((tm, tn), lambda i, j: (i, gf_col * nb + j))],
        out_specs=pl.BlockSpec((tm, tn), lambda i, j: (i, j)),
        out_shape=jax.ShapeDtypeStruct((M, N), BF16),
        compiler_params=_params(2), name="merge")(h, y, wm, wf, g_src, g_src)


def _ln(z, w, b):
    mu = jnp.mean(z, axis=-1, keepdims=True)
    d = z - mu
    var = jnp.mean(d * d, axis=-1, keepdims=True)
    return d * lax.rsqrt(var + LN_EPS) * w + b


def _mm_ln_kernel(a_ref, w_ref, x_ref, lw_ref, lb_ref, o_ref, ob_ref, *, alpha):
    hproj = jnp.dot(a_ref[...], w_ref[...], preferred_element_type=F32)
    out = _ln(alpha * x_ref[...] + hproj, lw_ref[...], lb_ref[...])
    o_ref[...] = out
    ob_ref[...] = out.astype(BF16)


def _mm_ln(a, w, xres, lw, lb, alpha, tm=256):
    M, K = a.shape
    N = w.shape[1]
    return pl.pallas_call(
        functools.partial(_mm_ln_kernel, alpha=alpha), grid=(M // tm,),
        in_specs=[pl.BlockSpec((tm, K), lambda i: (i, 0)),
                  pl.BlockSpec((K, N), lambda i: (0, 0)),
                  pl.BlockSpec((tm, N), lambda i: (i, 0)),
                  pl.BlockSpec((1, N), lambda i: (0, 0)),
                  pl.BlockSpec((1, N), lambda i: (0, 0))],
        out_specs=[pl.BlockSpec((tm, N), lambda i: (i, 0)),
                   pl.BlockSpec((tm, N), lambda i: (i, 0))],
        out_shape=[jax.ShapeDtypeStruct((M, N), F32), jax.ShapeDtypeStruct((M, N), BF16)],
        compiler_params=_params(1), name="proj_ln")(a, w, xres, lw, lb)


def _add_ln_kernel(ht_ref, x_ref, lw_ref, lb_ref, o_ref, *, alpha):
    o_ref[...] = _ln(alpha * x_ref[...] + ht_ref[...].T, lw_ref[...], lb_ref[...])


def _add_ln(ht, xres, lw, lb, alpha, tm=512):
    N, M = ht.shape
    return pl.pallas_call(
        functools.partial(_add_ln_kernel, alpha=alpha), grid=(M // tm,),
        in_specs=[pl.BlockSpec((N, tm), lambda i: (0, i)),
                  pl.BlockSpec((tm, N), lambda i: (i, 0)),
                  pl.BlockSpec((1, N), lambda i: (0, 0)),
                  pl.BlockSpec((1, N), lambda i: (0, 0))],
        out_specs=pl.BlockSpec((tm, N), lambda i: (i, 0)),
        out_shape=jax.ShapeDtypeStruct((M, N), F32),
        compiler_params=_params(1), name="add_ln")(ht, xres, lw, lb)


def _xattn_kernel(q_ref, kv_ref, o_ref, *, hd, d_model):
    scale = hd ** -0.5
    for h in range(X_HEADS):
        qh = q_ref[:, h * hd:(h + 1) * hd]
        kh = kv_ref[:, h * hd:(h + 1) * hd]
        vh = kv_ref[:, d_model + h * hd:d_model + (h + 1) * hd]
        s = lax.dot_general(qh, kh, _NT, preferred_element_type=F32) * scale
        m = jnp.max(s, axis=1, keepdims=True)
        p = jnp.exp(s - m)
        l = jnp.sum(p, axis=1, keepdims=True)
        o = jnp.dot(p.astype(BF16), vh, preferred_element_type=F32) / l
        o_ref[:, h * hd:(h + 1) * hd] = o.astype(o_ref.dtype)


def _xattn(q, kv, B, S, mem_len, tq=512):
    D = q.shape[1]
    nq = S // tq
    return pl.pallas_call(
        functools.partial(_xattn_kernel, hd=D // X_HEADS, d_model=D), grid=(B, nq),
        in_specs=[pl.BlockSpec((tq, D), lambda b, i: (b * nq + i, 0)),
                  pl.BlockSpec((mem_len, 2 * D), lambda b, i: (b, 0))],
        out_specs=pl.BlockSpec((tq, D), lambda b, i: (b * nq + i, 0)),
        out_shape=jax.ShapeDtypeStruct((B * S, D), BF16),
        compiler_params=_params(2), name="xattn")(q, kv)


def _top_values(work, k):
    vals = []
    for _ in range(k):
        m = jnp.max(work, axis=0, keepdims=True)
        vals.append(m)
        work = jnp.where(work == m, -jnp.inf, work)
    return vals


def _stack_rows(rows, n):
    ri = lax.broadcasted_iota(jnp.int32, (n, rows[0].shape[1]), 0)
    out = jnp.zeros((n, rows[0].shape[1]), F32)
    for r, v in enumerate(rows):
        out = jnp.where(ri == r, v, out)
    return out


def _pair_sums(t1, t2):
    t1s = _stack_rows(t1, PEER_TOPK)
    t2s = _stack_rows(t2, PEER_TOPK)
    r8 = lax.broadcasted_iota(jnp.int32, (8, t1s.shape[1]), 0)
    ninf = -jnp.inf
    pieces = [t1[0] + t2s,
              t1[1] + t2s[0:8],
              t1s[8:16] + t2[0],
              jnp.where(r8 < 5, t1[2] + t2s[0:8], ninf),
              jnp.where(r8 < 4, t1[3] + t2s[0:8], ninf),
              jnp.where(r8 < 3, t1[4] + t2s[0:8], ninf),
              jnp.where(r8 >= 5, t1s[0:8] + t2[0], ninf),
              jnp.where(r8 >= 5, t1s[0:8] + t2[1], ninf)]
    return jnp.concatenate(pieces, axis=0)


def _peer_score_kernel(q_ref, keys_ref, s1_ref, s2_ref, st_ref):
    dh = q_ref.shape[1] // (2 * PEER_HEADS)
    log2e = 1.0 / math.log(2.0)
    for h in range(PEER_HEADS):
        q1 = q_ref[:, (2 * h) * dh:(2 * h + 1) * dh]
        q2 = q_ref[:, (2 * h + 1) * dh:(2 * h + 2) * dh]
        s1 = lax.dot_general(keys_ref[2 * h], q1, _NT, preferred_element_type=F32) * log2e
        s2 = lax.dot_general(keys_ref[2 * h + 1], q2, _NT, preferred_element_type=F32) * log2e
        t1 = _top_values(s1, PEER_TOPK)
        t2 = _top_values(s2, PEER_TOPK)
        best = _top_values(_pair_sums(t1, t2), PEER_TOPK)
        m = best[0]
        z = jnp.zeros_like(m)
        for r in range(PEER_TOPK):
            z = z + jnp.exp2(best[r] - m)
        mz = m + jnp.log2(z) + 1.0
        tau = _top_values(_pair_sums([t - mz for t in t1], t2), PEER_TOPK)[PEER_TOPK - 1]
        s1_ref[h] = s1 - mz
        s2_ref[h] = s2
        st_ref[h] = jnp.broadcast_to(tau, (8, tau.shape[1]))


def _peer_scores(pq, keys, tm=256):
    M, W = pq.shape
    return pl.pallas_call(
        _peer_score_kernel, grid=(M // tm,),
        in_specs=[pl.BlockSpec((tm, W), lambda i: (i, 0)),
                  pl.BlockSpec(keys.shape, lambda i: (0, 0, 0))],
        out_specs=[pl.BlockSpec((PEER_HEADS, N_KEYS, tm), lambda i: (0, 0, i)),
                   pl.BlockSpec((PEER_HEADS, N_KEYS, tm), lambda i: (0, 0, i)),
                   pl.BlockSpec((PEER_HEADS, 8, tm), lambda i: (0, 0, i))],
        out_shape=[jax.ShapeDtypeStruct((PEER_HEADS, N_KEYS, M), F32),
                   jax.ShapeDtypeStruct((PEER_HEADS, N_KEYS, M), F32),
                   jax.ShapeDtypeStruct((PEER_HEADS, 8, M), F32)],
        compiler_params=_params(1), name="peer_scores")(pq, keys)


def _peer_mix_kernel(xt_ref, u_ref, vt_ref, s1_ref, s2_ref, st_ref, o_ref, a_sc, h_sc, *, ti, n_tiles, tn):
    g = pl.program_id(1)
    te = ti * N_KEYS

    @pl.when(g == 0)
    def _():
        o_ref[...] = jnp.zeros_like(o_ref)
        a_sc[...] = jnp.zeros_like(a_sc)
        h_sc[...] = jnp.zeros_like(h_sc)

    def stage_a(half, tok):
        return jnp.dot(u_ref[half * te:(half + 1) * te, :], xt_ref[:, tok],
                       preferred_element_type=F32)

    def rep(rows8):
        return jnp.concatenate([rows8] * (N_KEYS // 8), axis=0)

    def stage_b(a_t, tile, tok):
        tile = jnp.clip(tile, 0, n_tiles - 1)
        rows = []
        for il in range(ti):
            i = tile * ti + il
            acc = None
            for h in range(PEER_HEADS):
                d = s1_ref[h, pl.ds(i, 1), tok] + s2_ref[h, :, tok]
                w = jnp.where(d >= rep(st_ref[h, :, tok]), jnp.exp2(d), 0.0)
                acc = w if acc is None else acc + w
            rows.append(acc)
        half_gate = jnp.concatenate(rows, axis=0)
        erf_term = 1.0 + lax.erf(a_t * (1.0 / math.sqrt(2.0)))
        return (a_t * erf_term * half_gate).astype(BF16)

    for tb in range(xt_ref.shape[1] // tn):
        tok = slice(tb * tn, (tb + 1) * tn)
        a_even = stage_a(0, tok)
        h_odd = stage_b(a_sc[:, tok], 2 * g - 1, tok)
        h_pair = jnp.concatenate([h_sc[:, tok], h_odd], axis=0)
        o_ref[:, tok] += jnp.dot(vt_ref[...], h_pair, preferred_element_type=F32)
        a_sc[:, tok] = stage_a(1, tok)
        h_sc[:, tok] = stage_b(a_even, 2 * g, tok)


def _peer_mix(xbt, u, vt, s1, s2, st, tm=512, ti=4, tn=256):
    D, M = xbt.shape
    E = u.shape[0]
    te = ti * N_KEYS
    n_tiles = E // te
    n_pairs = n_tiles // 2
    return pl.pallas_call(
        functools.partial(_peer_mix_kernel, ti=ti, n_tiles=n_tiles, tn=tn), grid=(M // tm, n_pairs + 1),
        in_specs=[pl.BlockSpec((D, tm), lambda i, g: (0, i)),
                  pl.BlockSpec((2 * te, D), lambda i, g: (jnp.minimum(g, n_pairs - 1), 0)),
                  pl.BlockSpec((D, 2 * te), lambda i, g: (0, jnp.maximum(g - 1, 0))),
                  pl.BlockSpec((PEER_HEADS, N_KEYS, tm), lambda i, g: (0, 0, i)),
                  pl.BlockSpec((PEER_HEADS, N_KEYS, tm), lambda i, g: (0, 0, i)),
                  pl.BlockSpec((PEER_HEADS, 8, tm), lambda i, g: (0, 0, i))],
        out_specs=pl.BlockSpec((D, tm), lambda i, g: (0, i)),
        out_shape=jax.ShapeDtypeStruct((D, M), F32),
        scratch_shapes=[pltpu.VMEM((te, tm), F32), pltpu.VMEM((te, tm), BF16)],
        compiler_params=_params(2), name="peer_mix")(xbt, u, vt, s1, s2, st)


def _layer(xf, xb, mem_b, p, B, S, alpha):
    D = xf.shape[1]
    w_in = p["w_in"]
    qw = ML_HEADS * ML_QK
    vw = ML_HEADS * ML_V
    fw = FOX_HEADS * FOX_HD
    o = [0]
    for width in (qw, qw, vw, vw, ML_HEADS, ML_HEADS, fw, fw, fw, FOX_HEADS, D, D):
        o.append(o[-1] + width)
    seg = lambda a, b: w_in[:, o[a]:o[b]].astype(BF16)
    p_qk = _mm(xb, seg(0, 2), F32, 1024, 512, "in_proj_qk")
    p_v = _mm(xb, seg(2, 3), BF16, 1024, 512, "in_proj_v")
    p_o = _mm(xb, seg(3, 4), F32, 1024, 512, "in_proj_o")
    p_fx = _mm(xb, seg(6, 9), BF16, 1024, 512, "in_proj_fox")
    p_g = _mm(xb, seg(10, 12), F32, 1024, 512, "in_proj_gate")
    n_g = 2 * ML_HEADS + FOX_HEADS
    wgt = jnp.concatenate([w_in[:, o[4]:o[6]], w_in[:, o[9]:o[10]]], axis=1).T.astype(BF16)
    wgt = jnp.pad(wgt, ((0, N_GATE_ROWS - n_g), (0, 0)))
    gbias = jnp.concatenate([p["ml_gate_b"][0], p["ml_gate_b"][1], p["fox_f_b"],
                             jnp.zeros((N_GATE_ROWS - n_g,), F32)]).astype(F32)
    gbias = jnp.broadcast_to(gbias[:, None], (N_GATE_ROWS, LANES))
    gt = _gates(xb, wgt, gbias, B, S)

    col_scale = jnp.concatenate([jnp.ones((qw,), F32), jnp.full((qw,), ML_QK ** -0.5, F32)])[None, :]
    qk = _conv_silu(p_qk, p["ml_conv_w"], p["ml_conv_b"][None, :], col_scale, B, S)
    hg = _mlstm(qk, p_v, 0, p_o, 0, gt, p["ml_norm_w"][None, :], B, S)
    nfb = fw // FOX_HD
    y = _fox(p_fx, 0, nfb, 2 * nfb, gt.reshape(B, N_GATE_ROWS, 1, S), B, S)
    merged = _merge(hg, y, p["w_branch_ml"].astype(BF16), p["w_branch_fox"].astype(BF16), p_g, 0, 1)
    x1, x1b = _mm_ln(merged, p["w_out"].astype(BF16), xf, p["ln_w"][0:1], p["ln_b"][0:1], alpha)

    xq = _mm(x1b, p["w_xq"].astype(BF16), BF16, 1024, 512, "xattn_q")
    mem_len = mem_b.shape[0] // B
    kv = _mm(mem_b, p["w_xkv"].astype(BF16), BF16, 512, 512, "xattn_kv")
    xo = _xattn(xq, kv, B, S, mem_len)
    x2, x2b = _mm_ln(xo, p["w_xo"].astype(BF16), x1, p["ln_w"][1:2], p["ln_b"][1:2], alpha)

    pq = _mm(x2b, p["peer_wq"].astype(BF16), BF16, 1024, 512, "peer_q")
    keys = p["peer_sub_keys"].reshape(2 * PEER_HEADS, N_KEYS, -1).astype(BF16)
    s1, s2, st = _peer_scores(pq, keys)
    ht = _peer_mix(_transpose_cast(x2, BF16), p["peer_u"].astype(BF16), _transpose_cast(p["peer_v"], BF16),
                   s1, s2, st)
    x3 = _add_ln(ht, x2, p["ln_w"][2:3], p["ln_b"][2:3], alpha)
    return x3


def kernel(x, mem, w_in, ml_conv_w, ml_conv_b, ml_gate_b, ml_norm_w, fox_f_b, w_branch_ml, w_branch_fox,
           w_out, w_xq, w_xkv, w_xo, peer_wq, peer_sub_keys, peer_u, peer_v, ln_w, ln_b):
    B, S, D = x.shape
    depth = w_in.shape[0]
    alpha = (2 * depth) ** 0.25
    xf = x.reshape(B * S, D)
    mem_b = mem.reshape(-1, D).astype(BF16)
    for l in range(depth):
        p = dict(w_in=w_in[l], ml_conv_w=ml_conv_w[l], ml_conv_b=ml_conv_b[l], ml_gate_b=ml_gate_b[l],
                 ml_norm_w=ml_norm_w[l], fox_f_b=fox_f_b[l], w_branch_ml=w_branch_ml[l],
                 w_branch_fox=w_branch_fox[l], w_out=w_out[l], w_xq=w_xq[l], w_xkv=w_xkv[l], w_xo=w_xo[l],
                 peer_wq=peer_wq[l], peer_sub_keys=peer_sub_keys[l], peer_u=peer_u[l], peer_v=peer_v[l],
                 ln_w=ln_w[l], ln_b=ln_b[l])
        xf = _layer(xf, xf.astype(BF16), mem_b, p, B, S, alpha)
    return xf.reshape(B, S, D)
```

```python
import functools
import math

import jax
import jax.numpy as jnp
from jax import lax
from jax.experimental import pallas as pl
from jax.experimental.pallas import tpu as pltpu

F32 = jnp.float32
BF16 = jnp.bfloat16

ML_HEADS = 4
ML_QK = 256
ML_V = 512
CONV_W = 4
FOX_HEADS = 16
FOX_HD = 128
X_HEADS = 4
PEER_HEADS = 8
N_KEYS = 128
PEER_TOPK = 16
LN_EPS = 1e-5
N_GATE_ROWS = 32

VMEM_LIMIT_BYTES = 56 * 1024 * 1024
LANES = 128

_NT = (((1,), (1,)), ((), ()))
_TN = (((0,), (0,)), ((), ()))


def _params(n_axes):
    return pltpu.CompilerParams(dimension_semantics=("arbitrary",) * n_axes,
                                vmem_limit_bytes=VMEM_LIMIT_BYTES)


def _mm_kernel(x_ref, w_ref, o_ref):
    o_ref[...] = jnp.dot(x_ref[...], w_ref[...], preferred_element_type=F32).astype(o_ref.dtype)


def _mm(x, w, out_dtype, tm, tn, name):
    M, K = x.shape
    N = w.shape[1]
    return pl.pallas_call(
        _mm_kernel, grid=(M // tm, N // tn),
        in_specs=[pl.BlockSpec((tm, K), lambda i, j: (i, 0)),
                  pl.BlockSpec((K, tn), lambda i, j: (0, j))],
        out_specs=pl.BlockSpec((tm, tn), lambda i, j: (i, j)),
        out_shape=jax.ShapeDtypeStruct((M, N), out_dtype),
        compiler_params=_params(2), name=name)(x, w)


def _cast_cols_kernel(w_ref, wn_ref, o_ref, *, shift):
    w = w_ref[...]
    if shift:
        wide = jnp.concatenate([w, wn_ref[...]], axis=1)
        w = pltpu.roll(wide, wide.shape[1] - shift, axis=1)[:, :w.shape[1]]
    o_ref[...] = w.astype(o_ref.dtype)


def _cast_cols(w, col0, n_cols, tn=512):
    K = w.shape[0]
    base, shift = divmod(col0, tn)
    per = tn // LANES
    return pl.pallas_call(
        functools.partial(_cast_cols_kernel, shift=shift), grid=(n_cols // tn,),
        in_specs=[pl.BlockSpec((K, tn), lambda j: (0, base + j)),
                  pl.BlockSpec((K, LANES), lambda j: (0, (base + j + 1) * per))],
        out_specs=pl.BlockSpec((K, tn), lambda j: (0, j)),
        out_shape=jax.ShapeDtypeStruct((K, n_cols), BF16),
        compiler_params=_params(1), name="cast_cols")(w, w)


def _transpose_cast_kernel(x_ref, o_ref):
    o_ref[...] = x_ref[...].T.astype(o_ref.dtype)


def _transpose_cast(x, dtype, tr=512):
    R, C = x.shape
    return pl.pallas_call(
        _transpose_cast_kernel, grid=(R // tr,),
        in_specs=[pl.BlockSpec((tr, C), lambda i: (i, 0))],
        out_specs=pl.BlockSpec((C, tr), lambda i: (0, i)),
        out_shape=jax.ShapeDtypeStruct((C, R), dtype),
        compiler_params=_params(1), name="transpose_cast")(x)


def _gates_kernel(x_ref, wg_ref, b_ref, o_ref, carry_ref, *, ts):
    j = pl.program_id(1)

    @pl.when(j == 0)
    def _():
        carry_ref[...] = jnp.zeros_like(carry_ref)

    raw = lax.dot_general(wg_ref[...], x_ref[...], _NT, preferred_element_type=F32)
    pre = raw + b_ref[:, 0:1]
    ls = jnp.minimum(pre, 0.0) - jnp.log1p(jnp.exp(-jnp.abs(pre)))
    lane = lax.broadcasted_iota(jnp.int32, ls.shape, 1)
    c = ls
    sh = 1
    while sh < ts:
        c = c + jnp.where(lane >= sh, pltpu.roll(c, sh, axis=1), 0.0)
        sh *= 2
    c = c + carry_ref[:, 0:1]
    carry_ref[...] = jnp.broadcast_to(c[:, ts - 1:ts], carry_ref.shape)
    row = lax.broadcasted_iota(jnp.int32, ls.shape, 0)
    o_ref[0] = jnp.where(row < ML_HEADS, pre, c)


def _gates(xb, wgt, bias, B, S, ts=512):
    K = xb.shape[1]
    ns = S // ts
    return pl.pallas_call(
        functools.partial(_gates_kernel, ts=ts), grid=(B, ns),
        in_specs=[pl.BlockSpec((ts, K), lambda b, j: (b * ns + j, 0)),
                  pl.BlockSpec((N_GATE_ROWS, K), lambda b, j: (0, 0)),
                  pl.BlockSpec((N_GATE_ROWS, LANES), lambda b, j: (0, 0))],
        out_specs=pl.BlockSpec((1, N_GATE_ROWS, ts), lambda b, j: (b, 0, j)),
        out_shape=jax.ShapeDtypeStruct((B, N_GATE_ROWS, S), F32),
        scratch_shapes=[pltpu.VMEM((N_GATE_ROWS, LANES), F32)],
        compiler_params=_params(2), name="gates")(xb, wgt, bias)


def _conv_kernel(x_ref, w_ref, b_ref, sc_ref, o_ref):
    x = x_ref[...]
    rows = lax.broadcasted_iota(jnp.int32, x.shape, 0)
    y = x * w_ref[CONV_W - 1:CONV_W, :] + b_ref[...]
    for d in range(1, CONV_W):
        xs = jnp.where(rows >= d, pltpu.roll(x, d, axis=0), 0.0)
        y = y + xs * w_ref[CONV_W - 1 - d:CONV_W - d, :]
    y = y * jax.nn.sigmoid(y)
    o_ref[...] = (y * sc_ref[...]).astype(o_ref.dtype)


def _conv_silu(qk_src, conv_w, conv_b, col_scale, B, S, tc=256):
    C = conv_w.shape[1]
    return pl.pallas_call(
        _conv_kernel, grid=(B, C // tc),
        in_specs=[pl.BlockSpec((S, tc), lambda b, j: (b, j)),
                  pl.BlockSpec((CONV_W, tc), lambda b, j: (0, j)),
                  pl.BlockSpec((1, tc), lambda b, j: (0, j)),
                  pl.BlockSpec((1, tc), lambda b, j: (0, j))],
        out_specs=pl.BlockSpec((S, tc), lambda b, j: (b, j)),
        out_shape=jax.ShapeDtypeStruct((B * S, C), BF16),
        compiler_params=_params(2), name="conv_silu")(qk_src, conv_w, conv_b, col_scale)


def _row_to_col(row, eye):
    return jnp.sum(jnp.where(eye, row, 0.0), axis=1, keepdims=True)


def _mlstm_kernel(q_ref, k_ref, v_ref, og_ref, gt_ref, nw_ref, out_ref,
                  C_ref, n_ref, m_ref, g_ref, *, L):
    c = pl.program_id(1)

    @pl.when(c == 0)
    def _():
        C_ref[...] = jnp.zeros_like(C_ref)
        n_ref[...] = jnp.zeros_like(n_ref)
        m_ref[...] = jnp.zeros_like(m_ref)
        g_ref[...] = jnp.zeros_like(g_ref)

    ri = lax.broadcasted_iota(jnp.int32, (L, L), 0)
    ci = lax.broadcasted_iota(jnp.int32, (L, L), 1)
    eye = ri == ci
    causal = ci <= ri
    for h in range(ML_HEADS):
        ig = gt_ref[0, h:h + 1, :]
        gsum = gt_ref[0, ML_HEADS + h:ML_HEADS + h + 1, :]
        g0 = g_ref[h:h + 1, 0:1]
        m_prev = m_ref[h:h + 1, 0:1]
        b_row = gsum - g0
        b_last = b_row[:, L - 1:L]
        u_row = ig - b_row
        b_col = _row_to_col(b_row, eye)
        u_col = _row_to_col(u_row, eye)
        dlog = jnp.where(causal, b_col + u_row, -jnp.inf)
        a_col = b_col + m_prev
        m_t = jnp.maximum(a_col, jnp.max(dlog, axis=1, keepdims=True))
        qh = q_ref[:, h * ML_QK:(h + 1) * ML_QK]
        kh = k_ref[:, h * ML_QK:(h + 1) * ML_QK]
        vh = v_ref[:, h * ML_V:(h + 1) * ML_V]
        s = lax.dot_general(qh, kh, _NT, preferred_element_type=F32)
        p = jnp.exp(dlog - m_t) * s
        inter = jnp.exp(a_col - m_t)
        c_prev = C_ref[h]
        num = (jnp.dot(p.astype(BF16), vh, preferred_element_type=F32)
               + inter * jnp.dot(qh, c_prev.astype(BF16), preferred_element_type=F32))
        n_row = n_ref[h:h + 1, :]
        den = (jnp.sum(p, axis=1, keepdims=True)
               + inter * jnp.sum(qh.astype(F32) * n_row, axis=1, keepdims=True))
        ht = num / jnp.maximum(jnp.abs(den), jnp.exp(-m_t))
        w_end = b_last + u_col
        m_new = jnp.maximum(b_last + m_prev, jnp.max(w_end, axis=0, keepdims=True))
        decay = jnp.exp(b_last + m_prev - m_new)
        wk = jnp.exp(w_end - m_new) * kh.astype(F32)
        C_ref[h] = decay * c_prev + lax.dot_general(wk.astype(BF16), vh, _TN,
                                                    preferred_element_type=F32)
        n_ref[h:h + 1, :] = decay * n_row + jnp.sum(wk, axis=0, keepdims=True)
        m_ref[h:h + 1, :] = jnp.broadcast_to(m_new, (1, LANES))
        g_ref[h:h + 1, :] = jnp.broadcast_to(gsum[:, L - 1:L], (1, LANES))
        mu = jnp.mean(ht, axis=1, keepdims=True)
        d = ht - mu
        var = jnp.mean(d * d, axis=1, keepdims=True)
        hn = d * lax.rsqrt(var + LN_EPS) * nw_ref[:, h * ML_V:(h + 1) * ML_V]
        og = og_ref[:, h * ML_V:(h + 1) * ML_V]
        out_ref[:, h * ML_V:(h + 1) * ML_V] = (hn * jax.nn.sigmoid(og)).astype(out_ref.dtype)


def _mlstm(qk, vb_src, v_col, og_src, og_col, gt, norm_w, B, S, L=128):
    nc = S // L
    qw = ML_HEADS * ML_QK
    vw = ML_HEADS * ML_V
    return pl.pallas_call(
        functools.partial(_mlstm_kernel, L=L), grid=(B, nc),
        in_specs=[pl.BlockSpec((L, qw), lambda b, c: (b * nc + c, 0)),
                  pl.BlockSpec((L, qw), lambda b, c: (b * nc + c, 1)),
                  pl.BlockSpec((L, vw), lambda b, c: (b * nc + c, v_col)),
                  pl.BlockSpec((L, vw), lambda b, c: (b * nc + c, og_col)),
                  pl.BlockSpec((1, N_GATE_ROWS, L), lambda b, c: (b, 0, c)),
                  pl.BlockSpec((1, vw), lambda b, c: (0, 0))],
        out_specs=pl.BlockSpec((L, vw), lambda b, c: (b * nc + c, 0)),
        out_shape=jax.ShapeDtypeStruct((B * S, vw), BF16),
        scratch_shapes=[pltpu.VMEM((ML_HEADS, ML_QK, ML_V), F32),
                        pltpu.VMEM((8, ML_QK), F32),
                        pltpu.VMEM((8, LANES), F32),
                        pltpu.VMEM((8, LANES), F32)],
        compiler_params=_params(2), name="mlstm")(qk, qk, vb_src, og_src, gt, norm_w)


def _fox_kernel(q_ref, k_ref, v_ref, f_ref, o_ref, *, tq):
    S = q_ref.shape[0]
    scale = FOX_HD ** -0.5
    ri = lax.broadcasted_iota(jnp.int32, (tq, tq), 0)
    ci = lax.broadcasted_iota(jnp.int32, (tq, tq), 1)
    causal = ci <= ri
    for i in range(S // tq):
        lo = i * tq
        q = q_ref[lo:lo + tq, :]
        sd = lax.dot_general(q, k_ref[lo:lo + tq, :], _NT, preferred_element_type=F32) * scale
        sd = jnp.where(causal, sd - f_ref[0, 0, :, lo:lo + tq], -jnp.inf)
        m = jnp.max(sd, axis=1, keepdims=True)
        if i > 0:
            sl = lax.dot_general(q, k_ref[0:lo, :], _NT, preferred_element_type=F32) * scale
            sl = sl - f_ref[0, 0, :, 0:lo]
            m = jnp.maximum(m, jnp.max(sl, axis=1, keepdims=True))
        pd = jnp.exp(sd - m)
        l = jnp.sum(pd, axis=1, keepdims=True)
        o = jnp.dot(pd.astype(BF16), v_ref[lo:lo + tq, :], preferred_element_type=F32)
        if i > 0:
            pl_ = jnp.exp(sl - m)
            l = l + jnp.sum(pl_, axis=1, keepdims=True)
            o = o + jnp.dot(pl_.astype(BF16), v_ref[0:lo, :], preferred_element_type=F32)
        o_ref[lo:lo + tq, :] = (o / l).astype(o_ref.dtype)


def _fox(src, q_col, k_col, v_col, gt4, B, S, tq=256):
    frow = 2 * ML_HEADS
    return pl.pallas_call(
        functools.partial(_fox_kernel, tq=tq), grid=(B, FOX_HEADS),
        in_specs=[pl.BlockSpec((S, FOX_HD), lambda b, h: (b, q_col + h)),
                  pl.BlockSpec((S, FOX_HD), lambda b, h: (b, k_col + h)),
                  pl.BlockSpec((S, FOX_HD), lambda b, h: (b, v_col + h)),
                  pl.BlockSpec((1, 1, 1, S), lambda b, h: (b, frow + h, 0, 0))],
        out_specs=pl.BlockSpec((S, FOX_HD), lambda b, h: (b, h)),
        out_shape=jax.ShapeDtypeStruct((B * S, FOX_HEADS * FOX_HD), BF16),
        compiler_params=_params(2), name="fox_attn")(src, src, src, gt4)


def _merge_kernel(h_ref, y_ref, wm_ref, wf_ref, gm_ref, gf_ref, o_ref):
    a = jnp.dot(h_ref[...], wm_ref[...], preferred_element_type=F32)
    b = jnp.dot(y_ref[...], wf_ref[...], preferred_element_type=F32)
    o_ref[...] = (jax.nn.sigmoid(gm_ref[...]) * a + jax.nn.sigmoid(gf_ref[...]) * b).astype(o_ref.dtype)


def _merge(h, y, wm, wf, g_src, gm_col, gf_col, tm=1024, tn=512):
    M, K = h.shape
    N = wm.shape[1]
    nb = N // tn
    return pl.pallas_call(
        _merge_kernel, grid=(M // tm, nb),
        in_specs=[pl.BlockSpec((tm, K), lambda i, j: (i, 0)),
                  pl.BlockSpec((tm, K), lambda i, j: (i, 0)),
                  pl.BlockSpec((K, tn), lambda i, j: (0, j)),
                  pl.BlockSpec((K, tn), lambda i, j: (0, j)),
                  pl.BlockSpec((tm, tn), lambda i, j: (i, gm_col * nb + j)),
                  pl.BlockSpec((tm, tn), lambda i, j: (i, gf_col * nb + j))],
        out_specs=pl.BlockSpec((tm, tn), lambda i, j: (i, j)),
        out_shape=jax.ShapeDtypeStruct((M, N), BF16),
        compiler_params=_params(2), name="merge")(h, y, wm, wf, g_src, g_src)


def _ln(z, w, b):
    mu = jnp.mean(z, axis=-1, keepdims=True)
    d = z - mu
    var = jnp.mean(d * d, axis=-1, keepdims=True)
    return d * lax.rsqrt(var + LN_EPS) * w + b


def _mm_ln_kernel(a_ref, w_ref, x_ref, lw_ref, lb_ref, o_ref, ob_ref, *, alpha):
    hproj = jnp.dot(a_ref[...], w_ref[...], preferred_element_type=F32)
    out = _ln(alpha * x_ref[...] + hproj, lw_ref[...], lb_ref[...])
    o_ref[...] = out
    ob_ref[...] = out.astype(BF16)


def _mm_ln(a, w, xres, lw, lb, alpha, tm=256):
    M, K = a.shape
    N = w.shape[1]
    return pl.pallas_call(
        functools.partial(_mm_ln_kernel, alpha=alpha), grid=(M // tm,),
        in_specs=[pl.BlockSpec((tm, K), lambda i: (i, 0)),
                  pl.BlockSpec((K, N), lambda i: (0, 0)),
                  pl.BlockSpec((tm, N), lambda i: (i, 0)),
                  pl.BlockSpec((1, N), lambda i: (0, 0)),
                  pl.BlockSpec((1, N), lambda i: (0, 0))],
        out_specs=[pl.BlockSpec((tm, N), lambda i: (i, 0)),
                   pl.BlockSpec((tm, N), lambda i: (i, 0))],
        out_shape=[jax.ShapeDtypeStruct((M, N), F32), jax.ShapeDtypeStruct((M, N), BF16)],
        compiler_params=_params(1), name="proj_ln")(a, w, xres, lw, lb)


def _add_ln_kernel(ht_ref, x_ref, lw_ref, lb_ref, o_ref, *, alpha):
    o_ref[...] = _ln(alpha * x_ref[...] + ht_ref[...].T, lw_ref[...], lb_ref[...])


def _add_ln(ht, xres, lw, lb, alpha, tm=512):
    N, M = ht.shape
    return pl.pallas_call(
        functools.partial(_add_ln_kernel, alpha=alpha), grid=(M // tm,),
        in_specs=[pl.BlockSpec((N, tm), lambda i: (0, i)),
                  pl.BlockSpec((tm, N), lambda i: (i, 0)),
                  pl.BlockSpec((1, N), lambda i: (0, 0)),
                  pl.BlockSpec((1, N), lambda i: (0, 0))],
        out_specs=pl.BlockSpec((tm, N), lambda i: (i, 0)),
        out_shape=jax.ShapeDtypeStruct((M, N), F32),
        compiler_params=_params(1), name="add_ln")(ht, xres, lw, lb)


def _xattn_kernel(q_ref, kv_ref, o_ref, *, hd, d_model):
    scale = hd ** -0.5
    for h in range(X_HEADS):
        qh = q_ref[:, h * hd:(h + 1) * hd]
        kh = kv_ref[:, h * hd:(h + 1) * hd]
        vh = kv_ref[:, d_model + h * hd:d_model + (h + 1) * hd]
        s = lax.dot_general(qh, kh, _NT, preferred_element_type=F32) * scale
        m = jnp.max(s, axis=1, keepdims=True)
        p = jnp.exp(s - m)
        l = jnp.sum(p, axis=1, keepdims=True)
        o = jnp.dot(p.astype(BF16), vh, preferred_element_type=F32) / l
        o_ref[:, h * hd:(h + 1) * hd] = o.astype(o_ref.dtype)


def _xattn(q, kv, B, S, mem_len, tq=512):
    D = q.shape[1]
    nq = S // tq
    return pl.pallas_call(
        functools.partial(_xattn_kernel, hd=D // X_HEADS, d_model=D), grid=(B, nq),
        in_specs=[pl.BlockSpec((tq, D), lambda b, i: (b * nq + i, 0)),
                  pl.BlockSpec((mem_len, 2 * D), lambda b, i: (b, 0))],
        out_specs=pl.BlockSpec((tq, D), lambda b, i: (b * nq + i, 0)),
        out_shape=jax.ShapeDtypeStruct((B * S, D), BF16),
        compiler_params=_params(2), name="xattn")(q, kv)


def _top_values(work, k):
    vals = []
    for _ in range(k):
        m = jnp.max(work, axis=0, keepdims=True)
        vals.append(m)
        work = jnp.where(work == m, -jnp.inf, work)
    return vals


def _stack_rows(rows, n):
    ri = lax.broadcasted_iota(jnp.int32, (n, rows[0].shape[1]), 0)
    out = jnp.zeros((n, rows[0].shape[1]), F32)
    for r, v in enumerate(rows):
        out = jnp.where(ri == r, v, out)
    return out


def _pair_sums(t1, t2):
    t1s = _stack_rows(t1, PEER_TOPK)
    t2s = _stack_rows(t2, PEER_TOPK)
    r8 = lax.broadcasted_iota(jnp.int32, (8, t1s.shape[1]), 0)
    ninf = -jnp.inf
    pieces = [t1[0] + t2s,
              t1[1] + t2s[0:8],
              t1s[8:16] + t2[0],
              jnp.where(r8 < 5, t1[2] + t2s[0:8], ninf),
              jnp.where(r8 < 4, t1[3] + t2s[0:8], ninf),
              jnp.where(r8 < 3, t1[4] + t2s[0:8], ninf),
              jnp.where(r8 >= 5, t1s[0:8] + t2[0], ninf),
              jnp.where(r8 >= 5, t1s[0:8] + t2[1], ninf)]
    return jnp.concatenate(pieces, axis=0)


def _peer_score_kernel(q_ref, keys_ref, s1_ref, s2_ref, st_ref):
    dh = q_ref.shape[1] // (2 * PEER_HEADS)
    log2e = 1.0 / math.log(2.0)
    for h in range(PEER_HEADS):
        q1 = q_ref[:, (2 * h) * dh:(2 * h + 1) * dh]
        q2 = q_ref[:, (2 * h + 1) * dh:(2 * h + 2) * dh]
        s1 = lax.dot_general(keys_ref[2 * h], q1, _NT, preferred_element_type=F32) * log2e
        s2 = lax.dot_general(keys_ref[2 * h + 1], q2, _NT, preferred_element_type=F32) * log2e
        t1 = _top_values(s1, PEER_TOPK)
        t2 = _top_values(s2, PEER_TOPK)
        best = _top_values(_pair_sums(t1, t2), PEER_TOPK)
        m = best[0]
        z = jnp.zeros_like(m)
        for r in range(PEER_TOPK):
            z = z + jnp.exp2(best[r] - m)
        mz = m + jnp.log2(z) + 1.0
        tau = _top_values(_pair_sums([t - mz for t in t1], t2), PEER_TOPK)[PEER_TOPK - 1]
        s1_ref[h] = s1 - mz
        s2_ref[h] = s2
        st_ref[h] = jnp.broadcast_to(tau, (8, tau.shape[1]))


def _peer_scores(pq, keys, tm=256):
    M, W = pq.shape
    return pl.pallas_call(
        _peer_score_kernel, grid=(M // tm,),
        in_specs=[pl.BlockSpec((tm, W), lambda i: (i, 0)),
                  pl.BlockSpec(keys.shape, lambda i: (0, 0, 0))],
        out_specs=[pl.BlockSpec((PEER_HEADS, N_KEYS, tm), lambda i: (0, 0, i)),
                   pl.BlockSpec((PEER_HEADS, N_KEYS, tm), lambda i: (0, 0, i)),
                   pl.BlockSpec((PEER_HEADS, 8, tm), lambda i: (0, 0, i))],
        out_shape=[jax.ShapeDtypeStruct((PEER_HEADS, N_KEYS, M), F32),
                   jax.ShapeDtypeStruct((PEER_HEADS, N_KEYS, M), F32),
                   jax.ShapeDtypeStruct((PEER_HEADS, 8, M), F32)],
        compiler_params=_params(1), name="peer_scores")(pq, keys)


def _peer_mix_kernel(xt_ref, u_ref, vt_ref, s1_ref, s2_ref, st_ref, o_ref, a_sc, h_sc, *, ti, n_tiles, tn):
    g = pl.program_id(1)
    te = ti * N_KEYS

    @pl.when(g == 0)
    def _():
        o_ref[...] = jnp.zeros_like(o_ref)
        a_sc[...] = jnp.zeros_like(a_sc)
        h_sc[...] = jnp.zeros_like(h_sc)

    def stage_a(half, tok):
        return jnp.dot(u_ref[half * te:(half + 1) * te, :], xt_ref[:, tok],
                       preferred_element_type=F32)

    def rep(rows8):
        return jnp.concatenate([rows8] * (N_KEYS // 8), axis=0)

    def stage_b(a_t, tile, tok):
        tile = jnp.clip(tile, 0, n_tiles - 1)
        rows = []
        for il in range(ti):
            i = tile * ti + il
            acc = None
            for h in range(PEER_HEADS):
                d = s1_ref[h, pl.ds(i, 1), tok] + s2_ref[h, :, tok]
                w = jnp.where(d >= rep(st_ref[h, :, tok]), jnp.exp2(d), 0.0)
                acc = w if acc is None else acc + w
            rows.append(acc)
        half_gate = jnp.concatenate(rows, axis=0)
        erf_term = 1.0 + lax.erf(a_t * (1.0 / math.sqrt(2.0)))
        return (a_t * erf_term * half_gate).astype(BF16)

    for tb in range(xt_ref.shape[1] // tn):
        tok = slice(tb * tn, (tb + 1) * tn)
        a_even = stage_a(0, tok)
        h_odd = stage_b(a_sc[:, tok], 2 * g - 1, tok)
        h_pair = jnp.concatenate([h_sc[:, tok], h_odd], axis=0)
        o_ref[:, tok] += jnp.dot(vt_ref[...], h_pair, preferred_element_type=F32)
        a_sc[:, tok] = stage_a(1, tok)
        h_sc[:, tok] = stage_b(a_even, 2 * g, tok)


def _peer_mix(xbt, u, vt, s1, s2, st, tm=512, ti=4, tn=256):
    D, M = xbt.shape
    E = u.shape[0]
    te = ti * N_KEYS
    n_tiles = E // te
    n_pairs = n_tiles // 2
    return pl.pallas_call(
        functools.partial(_peer_mix_kernel, ti=ti, n_tiles=n_tiles, tn=tn), grid=(M // tm, n_pairs + 1),
        in_specs=[pl.BlockSpec((D, tm), lambda i, g: (0, i)),
                  pl.BlockSpec((2 * te, D), lambda i, g: (jnp.minimum(g, n_pairs - 1), 0)),
                  pl.BlockSpec((D, 2 * te), lambda i, g: (0, jnp.maximum(g - 1, 0))),
                  pl.BlockSpec((PEER_HEADS, N_KEYS, tm), lambda i, g: (0, 0, i)),
                  pl.BlockSpec((PEER_HEADS, N_KEYS, tm), lambda i, g: (0, 0, i)),
                  pl.BlockSpec((PEER_HEADS, 8, tm), lambda i, g: (0, 0, i))],
        out_specs=pl.BlockSpec((D, tm), lambda i, g: (0, i)),
        out_shape=jax.ShapeDtypeStruct((D, M), F32),
        scratch_shapes=[pltpu.VMEM((te, tm), F32), pltpu.VMEM((te, tm), BF16)],
        compiler_params=_params(2), name="peer_mix")(xbt, u, vt, s1, s2, st)


def _layer(xf, xb, mem_b, p, B, S, alpha):
    D = xf.shape[1]
    w_in = p["w_in"]
    qw = ML_HEADS * ML_QK
    vw = ML_HEADS * ML_V
    fw = FOX_HEADS * FOX_HD
    o = [0]
    for width in (qw, qw, vw, vw, ML_HEADS, ML_HEADS, fw, fw, fw, FOX_HEADS, D, D):
        o.append(o[-1] + width)
    seg = lambda a, b: _cast_cols(w_in, o[a], o[b] - o[a])
    p_qk = _mm(xb, seg(0, 2), F32, 1024, 1024, "in_proj_qk")
    p_v = _mm(xb, seg(2, 3), BF16, 1024, 1024, "in_proj_v")
    p_o = _mm(xb, seg(3, 4), F32, 1024, 1024, "in_proj_o")
    p_fx = _mm(xb, seg(6, 9), BF16, 1024, 1024, "in_proj_fox")
    p_g = _mm(xb, seg(10, 12), F32, 1024, 1024, "in_proj_gate")
    n_g = 2 * ML_HEADS + FOX_HEADS
    wgt = jnp.concatenate([w_in[:, o[4]:o[6]], w_in[:, o[9]:o[10]]], axis=1).T.astype(BF16)
    wgt = jnp.pad(wgt, ((0, N_GATE_ROWS - n_g), (0, 0)))
    gbias = jnp.concatenate([p["ml_gate_b"][0], p["ml_gate_b"][1], p["fox_f_b"],
                             jnp.zeros((N_GATE_ROWS - n_g,), F32)]).astype(F32)
    gbias = jnp.broadcast_to(gbias[:, None], (N_GATE_ROWS, LANES))
    gt = _gates(xb, wgt, gbias, B, S)

    col_scale = jnp.concatenate([jnp.ones((qw,), F32), jnp.full((qw,), ML_QK ** -0.5, F32)])[None, :]
    qk = _conv_silu(p_qk, p["ml_conv_w"], p["ml_conv_b"][None, :], col_scale, B, S)
    hg = _mlstm(qk, p_v, 0, p_o, 0, gt, p["ml_norm_w"][None, :], B, S)
    nfb = fw // FOX_HD
    y = _fox(p_fx, 0, nfb, 2 * nfb, gt.reshape(B, N_GATE_ROWS, 1, S), B, S)
    merged = _merge(hg, y, p["w_branch_ml"].astype(BF16), p["w_branch_fox"].astype(BF16), p_g, 0, 1)
    x1, x1b = _mm_ln(merged, p["w_out"].astype(BF16), xf, p["ln_w"][0:1], p["ln_b"][0:1], alpha)

    xq = _mm(x1b, p["w_xq"].astype(BF16), BF16, 1024, 512, "xattn_q")
    mem_len = mem_b.shape[0] // B
    kv = _mm(mem_b, p["w_xkv"].astype(BF16), BF16, 512, 512, "xattn_kv")
    xo = _xattn(xq, kv, B, S, mem_len)
    x2, x2b = _mm_ln(xo, p["w_xo"].astype(BF16), x1, p["ln_w"][1:2], p["ln_b"][1:2], alpha)

    pq = _mm(x2b, p["peer_wq"].astype(BF16), BF16, 1024, 512, "peer_q")
    keys = p["peer_sub_keys"].reshape(2 * PEER_HEADS, N_KEYS, -1).astype(BF16)
    s1, s2, st = _peer_scores(pq, keys)
    ht = _peer_mix(_transpose_cast(x2, BF16), p["peer_u"].astype(BF16), _transpose_cast(p["peer_v"], BF16),
                   s1, s2, st)
    x3 = _add_ln(ht, x2, p["ln_w"][2:3], p["ln_b"][2:3], alpha)
    return x3


def kernel(x, mem, w_in, ml_conv_w, ml_conv_b, ml_gate_b, ml_norm_w, fox_f_b, w_branch_ml, w_branch_fox,
           w_out, w_xq, w_xkv, w_xo, peer_wq, peer_sub_keys, peer_u, peer_v, ln_w, ln_b):
    B, S, D = x.shape
    depth = w_in.shape[0]
    alpha = (2 * depth) ** 0.25
    xf = x.reshape(B * S, D)
    mem_b = mem.reshape(-1, D).astype(BF16)
    for l in range(depth):
        p = dict(w_in=w_in[l], ml_conv_w=ml_conv_w[l], ml_conv_b=ml_conv_b[l], ml_gate_b=ml_gate_b[l],
                 ml_norm_w=ml_norm_w[l], fox_f_b=fox_f_b[l], w_branch_ml=w_branch_ml[l],
                 w_branch_fox=w_branch_fox[l], w_out=w_out[l], w_xq=w_xq[l], w_xkv=w_xkv[l], w_xo=w_xo[l],
                 peer_wq=peer_wq[l], peer_sub_keys=peer_sub_keys[l], peer_u=peer_u[l], peer_v=peer_v[l],
                 ln_w=ln_w[l], ln_b=ln_b[l])
        xf = _layer(xf, xf.astype(BF16), mem_b, p, B, S, alpha)
    return xf.reshape(B, S, D)
```

```python
import functools
import math

import jax
import jax.numpy as jnp
from jax import lax
from jax.experimental import pallas as pl
from jax.experimental.pallas import tpu as pltpu

F32 = jnp.float32
BF16 = jnp.bfloat16

ML_HEADS = 4
ML_QK = 256
ML_V = 512
CONV_W = 4
FOX_HEADS = 16
FOX_HD = 128
X_HEADS = 4
PEER_HEADS = 8
N_KEYS = 128
PEER_TOPK = 16
LN_EPS = 1e-5
N_GATE_ROWS = 32

VMEM_LIMIT_BYTES = 56 * 1024 * 1024
LANES = 128

_NT = (((1,), (1,)), ((), ()))
_TN = (((0,), (0,)), ((), ()))


def _params(n_axes):
    return pltpu.CompilerParams(dimension_semantics=("arbitrary",) * n_axes,
                                vmem_limit_bytes=VMEM_LIMIT_BYTES)


def _mm_kernel(x_ref, w_ref, o_ref):
    o_ref[...] = jnp.dot(x_ref[...], w_ref[...], preferred_element_type=F32).astype(o_ref.dtype)


def _mm(x, w, out_dtype, tm, tn, name):
    M, K = x.shape
    N = w.shape[1]
    return pl.pallas_call(
        _mm_kernel, grid=(M // tm, N // tn),
        in_specs=[pl.BlockSpec((tm, K), lambda i, j: (i, 0)),
                  pl.BlockSpec((K, tn), lambda i, j: (0, j))],
        out_specs=pl.BlockSpec((tm, tn), lambda i, j: (i, j)),
        out_shape=jax.ShapeDtypeStruct((M, N), out_dtype),
        compiler_params=_params(2), name=name)(x, w)


def _cast_cols_kernel(w_ref, wn_ref, o_ref, *, shift):
    w = w_ref[...]
    if shift:
        wide = jnp.concatenate([w, wn_ref[...]], axis=1)
        w = pltpu.roll(wide, wide.shape[1] - shift, axis=1)[:, :w.shape[1]]
    o_ref[...] = w.astype(o_ref.dtype)


def _cast_cols(w, col0, n_cols, tn=512):
    K = w.shape[0]
    base, shift = divmod(col0, tn)
    per = tn // LANES
    return pl.pallas_call(
        functools.partial(_cast_cols_kernel, shift=shift), grid=(n_cols // tn,),
        in_specs=[pl.BlockSpec((K, tn), lambda j: (0, base + j)),
                  pl.BlockSpec((K, LANES), lambda j: (0, (base + j + 1) * per))],
        out_specs=pl.BlockSpec((K, tn), lambda j: (0, j)),
        out_shape=jax.ShapeDtypeStruct((K, n_cols), BF16),
        compiler_params=_params(1), name="cast_cols")(w, w)


def _transpose_cast_kernel(x_ref, o_ref):
    o_ref[...] = x_ref[...].T.astype(o_ref.dtype)


def _transpose_cast(x, dtype, tr=512):
    R, C = x.shape
    return pl.pallas_call(
        _transpose_cast_kernel, grid=(R // tr,),
        in_specs=[pl.BlockSpec((tr, C), lambda i: (i, 0))],
        out_specs=pl.BlockSpec((C, tr), lambda i: (0, i)),
        out_shape=jax.ShapeDtypeStruct((C, R), dtype),
        compiler_params=_params(1), name="transpose_cast")(x)


def _gates_kernel(x_ref, wg_ref, b_ref, o_ref, carry_ref, *, ts):
    j = pl.program_id(1)

    @pl.when(j == 0)
    def _():
        carry_ref[...] = jnp.zeros_like(carry_ref)

    raw = lax.dot_general(wg_ref[...], x_ref[...], _NT, preferred_element_type=F32)
    pre = raw + b_ref[:, 0:1]
    ls = jnp.minimum(pre, 0.0) - jnp.log1p(jnp.exp(-jnp.abs(pre)))
    lane = lax.broadcasted_iota(jnp.int32, ls.shape, 1)
    c = ls
    sh = 1
    while sh < ts:
        c = c + jnp.where(lane >= sh, pltpu.roll(c, sh, axis=1), 0.0)
        sh *= 2
    c = c + carry_ref[:, 0:1]
    carry_ref[...] = jnp.broadcast_to(c[:, ts - 1:ts], carry_ref.shape)
    row = lax.broadcasted_iota(jnp.int32, ls.shape, 0)
    o_ref[0] = jnp.where(row < ML_HEADS, pre, c)


def _gates(xb, wgt, bias, B, S, ts=512):
    K = xb.shape[1]
    ns = S // ts
    return pl.pallas_call(
        functools.partial(_gates_kernel, ts=ts), grid=(B, ns),
        in_specs=[pl.BlockSpec((ts, K), lambda b, j: (b * ns + j, 0)),
                  pl.BlockSpec((N_GATE_ROWS, K), lambda b, j: (0, 0)),
                  pl.BlockSpec((N_GATE_ROWS, LANES), lambda b, j: (0, 0))],
        out_specs=pl.BlockSpec((1, N_GATE_ROWS, ts), lambda b, j: (b, 0, j)),
        out_shape=jax.ShapeDtypeStruct((B, N_GATE_ROWS, S), F32),
        scratch_shapes=[pltpu.VMEM((N_GATE_ROWS, LANES), F32)],
        compiler_params=_params(2), name="gates")(xb, wgt, bias)


def _conv_kernel(x_ref, w_ref, b_ref, sc_ref, o_ref):
    x = x_ref[...]
    rows = lax.broadcasted_iota(jnp.int32, x.shape, 0)
    y = x * w_ref[CONV_W - 1:CONV_W, :] + b_ref[...]
    for d in range(1, CONV_W):
        xs = jnp.where(rows >= d, pltpu.roll(x, d, axis=0), 0.0)
        y = y + xs * w_ref[CONV_W - 1 - d:CONV_W - d, :]
    y = y * jax.nn.sigmoid(y)
    o_ref[...] = (y * sc_ref[...]).astype(o_ref.dtype)


def _conv_silu(qk_src, conv_w, conv_b, col_scale, B, S, tc=256):
    C = conv_w.shape[1]
    return pl.pallas_call(
        _conv_kernel, grid=(B, C // tc),
        in_specs=[pl.BlockSpec((S, tc), lambda b, j: (b, j)),
                  pl.BlockSpec((CONV_W, tc), lambda b, j: (0, j)),
                  pl.BlockSpec((1, tc), lambda b, j: (0, j)),
                  pl.BlockSpec((1, tc), lambda b, j: (0, j))],
        out_specs=pl.BlockSpec((S, tc), lambda b, j: (b, j)),
        out_shape=jax.ShapeDtypeStruct((B * S, C), BF16),
        compiler_params=_params(2), name="conv_silu")(qk_src, conv_w, conv_b, col_scale)


def _row_to_col(row, eye):
    return jnp.sum(jnp.where(eye, row, 0.0), axis=1, keepdims=True)


def _mlstm_kernel(q_ref, k_ref, v_ref, og_ref, gt_ref, nw_ref, out_ref,
                  C_ref, n_ref, m_ref, g_ref, *, L):
    c = pl.program_id(1)

    @pl.when(c == 0)
    def _():
        C_ref[...] = jnp.zeros_like(C_ref)
        n_ref[...] = jnp.zeros_like(n_ref)
        m_ref[...] = jnp.zeros_like(m_ref)
        g_ref[...] = jnp.zeros_like(g_ref)

    ri = lax.broadcasted_iota(jnp.int32, (L, L), 0)
    ci = lax.broadcasted_iota(jnp.int32, (L, L), 1)
    eye = ri == ci
    causal = ci <= ri
    for h in range(ML_HEADS):
        ig = gt_ref[0, h:h + 1, :]
        gsum = gt_ref[0, ML_HEADS + h:ML_HEADS + h + 1, :]
        g0 = g_ref[h:h + 1, 0:1]
        m_prev = m_ref[h:h + 1, 0:1]
        b_row = gsum - g0
        b_last = b_row[:, L - 1:L]
        u_row = ig - b_row
        b_col = _row_to_col(b_row, eye)
        u_col = _row_to_col(u_row, eye)
        dlog = jnp.where(causal, b_col + u_row, -jnp.inf)
        a_col = b_col + m_prev
        m_t = jnp.maximum(a_col, jnp.max(dlog, axis=1, keepdims=True))
        qh = q_ref[:, h * ML_QK:(h + 1) * ML_QK]
        kh = k_ref[:, h * ML_QK:(h + 1) * ML_QK]
        vh = v_ref[:, h * ML_V:(h + 1) * ML_V]
        s = lax.dot_general(qh, kh, _NT, preferred_element_type=F32)
        p = jnp.exp(dlog - m_t) * s
        inter = jnp.exp(a_col - m_t)
        c_prev = C_ref[h]
        num = (jnp.dot(p.astype(BF16), vh, preferred_element_type=F32)
               + inter * jnp.dot(qh, c_prev.astype(BF16), preferred_element_type=F32))
        n_row = n_ref[h:h + 1, :]
        den = (jnp.sum(p, axis=1, keepdims=True)
               + inter * jnp.sum(qh.astype(F32) * n_row, axis=1, keepdims=True))
        ht = num / jnp.maximum(jnp.abs(den), jnp.exp(-m_t))
        w_end = b_last + u_col
        m_new = jnp.maximum(b_last + m_prev, jnp.max(w_end, axis=0, keepdims=True))
        decay = jnp.exp(b_last + m_prev - m_new)
        wk = jnp.exp(w_end - m_new) * kh.astype(F32)
        C_ref[h] = decay * c_prev + lax.dot_general(wk.astype(BF16), vh, _TN,
                                                    preferred_element_type=F32)
        n_ref[h:h + 1, :] = decay * n_row + jnp.sum(wk, axis=0, keepdims=True)
        m_ref[h:h + 1, :] = jnp.broadcast_to(m_new, (1, LANES))
        g_ref[h:h + 1, :] = jnp.broadcast_to(gsum[:, L - 1:L], (1, LANES))
        mu = jnp.mean(ht, axis=1, keepdims=True)
        d = ht - mu
        var = jnp.mean(d * d, axis=1, keepdims=True)
        hn = d * lax.rsqrt(var + LN_EPS) * nw_ref[:, h * ML_V:(h + 1) * ML_V]
        og = og_ref[:, h * ML_V:(h + 1) * ML_V]
        out_ref[:, h * ML_V:(h + 1) * ML_V] = (hn * jax.nn.sigmoid(og)).astype(out_ref.dtype)


def _mlstm(qk, vb_src, v_col, og_src, og_col, gt, norm_w, B, S, L=128):
    nc = S // L
    qw = ML_HEADS * ML_QK
    vw = ML_HEADS * ML_V
    return pl.pallas_call(
        functools.partial(_mlstm_kernel, L=L), grid=(B, nc),
        in_specs=[pl.BlockSpec((L, qw), lambda b, c: (b * nc + c, 0)),
                  pl.BlockSpec((L, qw), lambda b, c: (b * nc + c, 1)),
                  pl.BlockSpec((L, vw), lambda b, c: (b * nc + c, v_col)),
                  pl.BlockSpec((L, vw), lambda b, c: (b * nc + c, og_col)),
                  pl.BlockSpec((1, N_GATE_ROWS, L), lambda b, c: (b, 0, c)),
                  pl.BlockSpec((1, vw), lambda b, c: (0, 0))],
        out_specs=pl.BlockSpec((L, vw), lambda b, c: (b * nc + c, 0)),
        out_shape=jax.ShapeDtypeStruct((B * S, vw), BF16),
        scratch_shapes=[pltpu.VMEM((ML_HEADS, ML_QK, ML_V), F32),
                        pltpu.VMEM((8, ML_QK), F32),
                        pltpu.VMEM((8, LANES), F32),
                        pltpu.VMEM((8, LANES), F32)],
        compiler_params=_params(2), name="mlstm")(qk, qk, vb_src, og_src, gt, norm_w)


def _fox_kernel(q_ref, k_ref, v_ref, f_ref, o_ref, *, tq):
    S = q_ref.shape[0]
    scale = FOX_HD ** -0.5
    ri = lax.broadcasted_iota(jnp.int32, (tq, tq), 0)
    ci = lax.broadcasted_iota(jnp.int32, (tq, tq), 1)
    causal = ci <= ri
    for i in range(S // tq):
        lo = i * tq
        q = q_ref[lo:lo + tq, :]
        sd = lax.dot_general(q, k_ref[lo:lo + tq, :], _NT, preferred_element_type=F32) * scale
        sd = jnp.where(causal, sd - f_ref[0, 0, :, lo:lo + tq], -jnp.inf)
        m = jnp.max(sd, axis=1, keepdims=True)
        if i > 0:
            sl = lax.dot_general(q, k_ref[0:lo, :], _NT, preferred_element_type=F32) * scale
            sl = sl - f_ref[0, 0, :, 0:lo]
            m = jnp.maximum(m, jnp.max(sl, axis=1, keepdims=True))
        pd = jnp.exp(sd - m)
        l = jnp.sum(pd, axis=1, keepdims=True)
        o = jnp.dot(pd.astype(BF16), v_ref[lo:lo + tq, :], preferred_element_type=F32)
        if i > 0:
            pl_ = jnp.exp(sl - m)
            l = l + jnp.sum(pl_, axis=1, keepdims=True)
            o = o + jnp.dot(pl_.astype(BF16), v_ref[0:lo, :], preferred_element_type=F32)
        o_ref[lo:lo + tq, :] = (o / l).astype(o_ref.dtype)


def _fox(src, q_col, k_col, v_col, gt4, B, S, tq=256):
    frow = 2 * ML_HEADS
    return pl.pallas_call(
        functools.partial(_fox_kernel, tq=tq), grid=(B, FOX_HEADS),
        in_specs=[pl.BlockSpec((S, FOX_HD), lambda b, h: (b, q_col + h)),
                  pl.BlockSpec((S, FOX_HD), lambda b, h: (b, k_col + h)),
                  pl.BlockSpec((S, FOX_HD), lambda b, h: (b, v_col + h)),
                  pl.BlockSpec((1, 1, 1, S), lambda b, h: (b, frow + h, 0, 0))],
        out_specs=pl.BlockSpec((S, FOX_HD), lambda b, h: (b, h)),
        out_shape=jax.ShapeDtypeStruct((B * S, FOX_HEADS * FOX_HD), BF16),
        compiler_params=_params(2), name="fox_attn")(src, src, src, gt4)


def _merge_kernel(h_ref, y_ref, wm_ref, wf_ref, gm_ref, gf_ref, o_ref):
    a = jnp.dot(h_ref[...], wm_ref[...], preferred_element_type=F32)
    b = jnp.dot(y_ref[...], wf_ref[...], preferred_element_type=F32)
    o_ref[...] = (jax.nn.sigmoid(gm_ref[...]) * a + jax.nn.sigmoid(gf_ref[...]) * b).astype(o_ref.dtype)


def _merge(h, y, wm, wf, g_src, gm_col, gf_col, tm=1024, tn=512):
    M, K = h.shape
    N = wm.shape[1]
    nb = N // tn
    return pl.pallas_call(
        _merge_kernel, grid=(M // tm, nb),
        in_specs=[pl.BlockSpec((tm, K), lambda i, j: (i, 0)),
                  pl.BlockSpec((tm, K), lambda i, j: (i, 0)),
                  pl.BlockSpec((K, tn), lambda i, j: (0, j)),
                  pl.BlockSpec((K, tn), lambda i, j: (0, j)),
                  pl.BlockSpec((tm, tn), lambda i, j: (i, gm_col * nb + j)),
                  pl.BlockSpec((tm, tn), lambda i, j: (i, gf_col * nb + j))],
        out_specs=pl.BlockSpec((tm, tn), lambda i, j: (i, j)),
        out_shape=jax.ShapeDtypeStruct((M, N), BF16),
        compiler_params=_params(2), name="merge")(h, y, wm, wf, g_src, g_src)


def _ln(z, w, b):
    mu = jnp.mean(z, axis=-1, keepdims=True)
    d = z - mu
    var = jnp.mean(d * d, axis=-1, keepdims=True)
    return d * lax.rsqrt(var + LN_EPS) * w + b


def _mm_ln_kernel(a_ref, w_ref, x_ref, lw_ref, lb_ref, o_ref, ob_ref, *, alpha):
    hproj = jnp.dot(a_ref[...], w_ref[...], preferred_element_type=F32)
    out = _ln(alpha * x_ref[...] + hproj, lw_ref[...], lb_ref[...])
    o_ref[...] = out
    ob_ref[...] = out.astype(BF16)


def _mm_ln(a, w, xres, lw, lb, alpha, tm=512):
    M, K = a.shape
    N = w.shape[1]
    return pl.pallas_call(
        functools.partial(_mm_ln_kernel, alpha=alpha), grid=(M // tm,),
        in_specs=[pl.BlockSpec((tm, K), lambda i: (i, 0)),
                  pl.BlockSpec((K, N), lambda i: (0, 0)),
                  pl.BlockSpec((tm, N), lambda i: (i, 0)),
                  pl.BlockSpec((1, N), lambda i: (0, 0)),
                  pl.BlockSpec((1, N), lambda i: (0, 0))],
        out_specs=[pl.BlockSpec((tm, N), lambda i: (i, 0)),
                   pl.BlockSpec((tm, N), lambda i: (i, 0))],
        out_shape=[jax.ShapeDtypeStruct((M, N), F32), jax.ShapeDtypeStruct((M, N), BF16)],
        compiler_params=_params(1), name="proj_ln")(a, w, xres, lw, lb)


def _add_ln_kernel(ht_ref, x_ref, lw_ref, lb_ref, o_ref, *, alpha):
    o_ref[...] = _ln(alpha * x_ref[...] + ht_ref[...].T, lw_ref[...], lb_ref[...])


def _add_ln(ht, xres, lw, lb, alpha, tm=512):
    N, M = ht.shape
    return pl.pallas_call(
        functools.partial(_add_ln_kernel, alpha=alpha), grid=(M // tm,),
        in_specs=[pl.BlockSpec((N, tm), lambda i: (0, i)),
                  pl.BlockSpec((tm, N), lambda i: (i, 0)),
                  pl.BlockSpec((1, N), lambda i: (0, 0)),
                  pl.BlockSpec((1, N), lambda i: (0, 0))],
        out_specs=pl.BlockSpec((tm, N), lambda i: (i, 0)),
        out_shape=jax.ShapeDtypeStruct((M, N), F32),
        compiler_params=_params(1), name="add_ln")(ht, xres, lw, lb)


def _xattn_kernel(q_ref, kv_ref, o_ref, *, hd, d_model):
    scale = hd ** -0.5
    for h in range(X_HEADS):
        qh = q_ref[:, h * hd:(h + 1) * hd]
        kh = kv_ref[:, h * hd:(h + 1) * hd]
        vh = kv_ref[:, d_model + h * hd:d_model + (h + 1) * hd]
        s = lax.dot_general(qh, kh, _NT, preferred_element_type=F32) * scale
        m = jnp.max(s, axis=1, keepdims=True)
        p = jnp.exp(s - m)
        l = jnp.sum(p, axis=1, keepdims=True)
        o = jnp.dot(p.astype(BF16), vh, preferred_element_type=F32) / l
        o_ref[:, h * hd:(h + 1) * hd] = o.astype(o_ref.dtype)


def _xattn(q, kv, B, S, mem_len, tq=512):
    D = q.shape[1]
    nq = S // tq
    return pl.pallas_call(
        functools.partial(_xattn_kernel, hd=D // X_HEADS, d_model=D), grid=(B, nq),
        in_specs=[pl.BlockSpec((tq, D), lambda b, i: (b * nq + i, 0)),
                  pl.BlockSpec((mem_len, 2 * D), lambda b, i: (b, 0))],
        out_specs=pl.BlockSpec((tq, D), lambda b, i: (b * nq + i, 0)),
        out_shape=jax.ShapeDtypeStruct((B * S, D), BF16),
        compiler_params=_params(2), name="xattn")(q, kv)


def _top_values(work, k):
    vals = []
    for _ in range(k):
        m = jnp.max(work, axis=0, keepdims=True)
        vals.append(m)
        work = jnp.where(work == m, -jnp.inf, work)
    return vals


def _stack_rows(rows, n):
    ri = lax.broadcasted_iota(jnp.int32, (n, rows[0].shape[1]), 0)
    out = jnp.zeros((n, rows[0].shape[1]), F32)
    for r, v in enumerate(rows):
        out = jnp.where(ri == r, v, out)
    return out


def _pair_sums(t1, t2):
    t1s = _stack_rows(t1, PEER_TOPK)
    t2s = _stack_rows(t2, PEER_TOPK)
    r8 = lax.broadcasted_iota(jnp.int32, (8, t1s.shape[1]), 0)
    ninf = -jnp.inf
    pieces = [t1[0] + t2s,
              t1[1] + t2s[0:8],
              t1s[8:16] + t2[0],
              jnp.where(r8 < 5, t1[2] + t2s[0:8], ninf),
              jnp.where(r8 < 4, t1[3] + t2s[0:8], ninf),
              jnp.where(r8 < 3, t1[4] + t2s[0:8], ninf),
              jnp.where(r8 >= 5, t1s[0:8] + t2[0], ninf),
              jnp.where(r8 >= 5, t1s[0:8] + t2[1], ninf)]
    return jnp.concatenate(pieces, axis=0)


def _peer_score_kernel(q_ref, keys_ref, s1_ref, s2_ref, st_ref):
    dh = q_ref.shape[1] // (2 * PEER_HEADS)
    log2e = 1.0 / math.log(2.0)
    for h in range(PEER_HEADS):
        q1 = q_ref[:, (2 * h) * dh:(2 * h + 1) * dh]
        q2 = q_ref[:, (2 * h + 1) * dh:(2 * h + 2) * dh]
        s1 = lax.dot_general(keys_ref[2 * h], q1, _NT, preferred_element_type=F32) * log2e
        s2 = lax.dot_general(keys_ref[2 * h + 1], q2, _NT, preferred_element_type=F32) * log2e
        t1 = _top_values(s1, PEER_TOPK)
        t2 = _top_values(s2, PEER_TOPK)
        best = _top_values(_pair_sums(t1, t2), PEER_TOPK)
        m = best[0]
        z = jnp.zeros_like(m)
        for r in range(PEER_TOPK):
            z = z + jnp.exp2(best[r] - m)
        mz = m + jnp.log2(z) + 1.0
        tau = _top_values(_pair_sums([t - mz for t in t1], t2), PEER_TOPK)[PEER_TOPK - 1]
        s1_ref[h] = s1 - mz
        s2_ref[h] = s2
        st_ref[h] = jnp.broadcast_to(tau, (8, tau.shape[1]))


def _peer_scores(pq, keys, tm=256):
    M, W = pq.shape
    return pl.pallas_call(
        _peer_score_kernel, grid=(M // tm,),
        in_specs=[pl.BlockSpec((tm, W), lambda i: (i, 0)),
                  pl.BlockSpec(keys.shape, lambda i: (0, 0, 0))],
        out_specs=[pl.BlockSpec((PEER_HEADS, N_KEYS, tm), lambda i: (0, 0, i)),
                   pl.BlockSpec((PEER_HEADS, N_KEYS, tm), lambda i: (0, 0, i)),
                   pl.BlockSpec((PEER_HEADS, 8, tm), lambda i: (0, 0, i))],
        out_shape=[jax.ShapeDtypeStruct((PEER_HEADS, N_KEYS, M), F32),
                   jax.ShapeDtypeStruct((PEER_HEADS, N_KEYS, M), F32),
                   jax.ShapeDtypeStruct((PEER_HEADS, 8, M), F32)],
        compiler_params=_params(1), name="peer_scores")(pq, keys)


def _peer_mix_kernel(xt_ref, u_ref, vt_ref, s1_ref, s2_ref, st_ref, o_ref, a_sc, h_sc, *, ti, n_tiles, tn):
    g = pl.program_id(1)
    te = ti * N_KEYS

    @pl.when(g == 0)
    def _():
        o_ref[...] = jnp.zeros_like(o_ref)
        a_sc[...] = jnp.zeros_like(a_sc)
        h_sc[...] = jnp.zeros_like(h_sc)

    def stage_a(half, tok):
        return jnp.dot(u_ref[half * te:(half + 1) * te, :], xt_ref[:, tok],
                       preferred_element_type=F32)

    def rep(rows8):
        return jnp.concatenate([rows8] * (N_KEYS // 8), axis=0)

    def stage_b(a_t, tile, tok):
        tile = jnp.clip(tile, 0, n_tiles - 1)
        rows = []
        for il in range(ti):
            i = tile * ti + il
            acc = None
            for h in range(PEER_HEADS):
                d = s1_ref[h, pl.ds(i, 1), tok] + s2_ref[h, :, tok]
                w = jnp.where(d >= rep(st_ref[h, :, tok]), jnp.exp2(d), 0.0)
                acc = w if acc is None else acc + w
            rows.append(acc)
        half_gate = jnp.concatenate(rows, axis=0)
        erf_term = 1.0 + lax.erf(a_t * (1.0 / math.sqrt(2.0)))
        return (a_t * erf_term * half_gate).astype(BF16)

    for tb in range(xt_ref.shape[1] // tn):
        tok = slice(tb * tn, (tb + 1) * tn)
        a_even = stage_a(0, tok)
        h_odd = stage_b(a_sc[:, tok], 2 * g - 1, tok)
        h_pair = jnp.concatenate([h_sc[:, tok], h_odd], axis=0)
        o_ref[:, tok] += jnp.dot(vt_ref[...], h_pair, preferred_element_type=F32)
        a_sc[:, tok] = stage_a(1, tok)
        h_sc[:, tok] = stage_b(a_even, 2 * g, tok)


def _peer_mix(xbt, u, vt, s1, s2, st, tm=512, ti=4, tn=256):
    D, M = xbt.shape
    E = u.shape[0]
    te = ti * N_KEYS
    n_tiles = E // te
    n_pairs = n_tiles // 2
    return pl.pallas_call(
        functools.partial(_peer_mix_kernel, ti=ti, n_tiles=n_tiles, tn=tn), grid=(M // tm, n_pairs + 1),
        in_specs=[pl.BlockSpec((D, tm), lambda i, g: (0, i)),
                  pl.BlockSpec((2 * te, D), lambda i, g: (jnp.minimum(g, n_pairs - 1), 0)),
                  pl.BlockSpec((D, 2 * te), lambda i, g: (0, jnp.maximum(g - 1, 0))),
                  pl.BlockSpec((PEER_HEADS, N_KEYS, tm), lambda i, g: (0, 0, i)),
                  pl.BlockSpec((PEER_HEADS, N_KEYS, tm), lambda i, g: (0, 0, i)),
                  pl.BlockSpec((PEER_HEADS, 8, tm), lambda i, g: (0, 0, i))],
        out_specs=pl.BlockSpec((D, tm), lambda i, g: (0, i)),
        out_shape=jax.ShapeDtypeStruct((D, M), F32),
        scratch_shapes=[pltpu.VMEM((te, tm), F32), pltpu.VMEM((te, tm), BF16)],
        compiler_params=_params(2), name="peer_mix")(xbt, u, vt, s1, s2, st)


def _layer(xf, xb, mem_b, p, B, S, alpha):
    D = xf.shape[1]
    w_in = p["w_in"]
    qw = ML_HEADS * ML_QK
    vw = ML_HEADS * ML_V
    fw = FOX_HEADS * FOX_HD
    o = [0]
    for width in (qw, qw, vw, vw, ML_HEADS, ML_HEADS, fw, fw, fw, FOX_HEADS, D, D):
        o.append(o[-1] + width)
    seg = lambda a, b: _cast_cols(w_in, o[a], o[b] - o[a])
    p_qk = _mm(xb, seg(0, 2), F32, 1024, 1024, "in_proj_qk")
    p_v = _mm(xb, seg(2, 3), BF16, 1024, 1024, "in_proj_v")
    p_o = _mm(xb, seg(3, 4), F32, 1024, 1024, "in_proj_o")
    p_fx = _mm(xb, seg(6, 9), BF16, 1024, 1024, "in_proj_fox")
    p_g = _mm(xb, seg(10, 12), F32, 1024, 1024, "in_proj_gate")
    n_g = 2 * ML_HEADS + FOX_HEADS
    ml_lo, fx_lo = (o[4] // LANES) * LANES, (o[9] // LANES) * LANES
    ml_win = _cast_cols(w_in, ml_lo, LANES, tn=LANES)
    fx_win = _cast_cols(w_in, fx_lo, LANES, tn=LANES)
    wgt = jnp.concatenate([ml_win[:, o[4] - ml_lo:o[6] - ml_lo], fx_win[:, o[9] - fx_lo:o[10] - fx_lo]], axis=1).T
    wgt = jnp.pad(wgt, ((0, N_GATE_ROWS - n_g), (0, 0)))
    gbias = jnp.concatenate([p["ml_gate_b"][0], p["ml_gate_b"][1], p["fox_f_b"],
                             jnp.zeros((N_GATE_ROWS - n_g,), F32)]).astype(F32)
    gbias = jnp.broadcast_to(gbias[:, None], (N_GATE_ROWS, LANES))
    gt = _gates(xb, wgt, gbias, B, S)

    col_scale = jnp.concatenate([jnp.ones((qw,), F32), jnp.full((qw,), ML_QK ** -0.5, F32)])[None, :]
    qk = _conv_silu(p_qk, p["ml_conv_w"], p["ml_conv_b"][None, :], col_scale, B, S)
    hg = _mlstm(qk, p_v, 0, p_o, 0, gt, p["ml_norm_w"][None, :], B, S)
    nfb = fw // FOX_HD
    y = _fox(p_fx, 0, nfb, 2 * nfb, gt.reshape(B, N_GATE_ROWS, 1, S), B, S)
    merged = _merge(hg, y, p["w_branch_ml"].astype(BF16), p["w_branch_fox"].astype(BF16), p_g, 0, 1)
    x1, x1b = _mm_ln(merged, p["w_out"].astype(BF16), xf, p["ln_w"][0:1], p["ln_b"][0:1], alpha)

    xq = _mm(x1b, p["w_xq"].astype(BF16), BF16, 1024, 512, "xattn_q")
    mem_len = mem_b.shape[0] // B
    kv = _mm(mem_b, p["w_xkv"].astype(BF16), BF16, 512, 512, "xattn_kv")
    xo = _xattn(xq, kv, B, S, mem_len)
    x2, x2b = _mm_ln(xo, p["w_xo"].astype(BF16), x1, p["ln_w"][1:2], p["ln_b"][1:2], alpha)

    pq = _mm(x2b, p["peer_wq"].astype(BF16), BF16, 1024, 512, "peer_q")
    keys = p["peer_sub_keys"].reshape(2 * PEER_HEADS, N_KEYS, -1).astype(BF16)
    s1, s2, st = _peer_scores(pq, keys)
    ht = _peer_mix(_transpose_cast(x2, BF16), p["peer_u"].astype(BF16), _transpose_cast(p["peer_v"], BF16),
                   s1, s2, st)
    x3 = _add_ln(ht, x2, p["ln_w"][2:3], p["ln_b"][2:3], alpha)
    return x3


def kernel(x, mem, w_in, ml_conv_w, ml_conv_b, ml_gate_b, ml_norm_w, fox_f_b, w_branch_ml, w_branch_fox,
           w_out, w_xq, w_xkv, w_xo, peer_wq, peer_sub_keys, peer_u, peer_v, ln_w, ln_b):
    B, S, D = x.shape
    depth = w_in.shape[0]
    alpha = (2 * depth) ** 0.25
    xf = x.reshape(B * S, D)
    mem_b = mem.reshape(-1, D).astype(BF16)
    for l in range(depth):
        p = dict(w_in=w_in[l], ml_conv_w=ml_conv_w[l], ml_conv_b=ml_conv_b[l], ml_gate_b=ml_gate_b[l],
                 ml_norm_w=ml_norm_w[l], fox_f_b=fox_f_b[l], w_branch_ml=w_branch_ml[l],
                 w_branch_fox=w_branch_fox[l], w_out=w_out[l], w_xq=w_xq[l], w_xkv=w_xkv[l], w_xo=w_xo[l],
                 peer_wq=peer_wq[l], peer_sub_keys=peer_sub_keys[l], peer_u=peer_u[l], peer_v=peer_v[l],
                 ln_w=ln_w[l], ln_b=ln_b[l])
        xf = _layer(xf, xf.astype(BF16), mem_b, p, B, S, alpha)
    return xf.reshape(B, S, D)
```

```python
import functools
import math

import jax
import jax.numpy as jnp
from jax import lax
from jax.experimental import pallas as pl
from jax.experimental.pallas import tpu as pltpu

F32 = jnp.float32
BF16 = jnp.bfloat16

ML_HEADS = 4
ML_QK = 256
ML_V = 512
CONV_W = 4
FOX_HEADS = 16
FOX_HD = 128
X_HEADS = 4
PEER_HEADS = 8
N_KEYS = 128
PEER_TOPK = 16
LN_EPS = 1e-5
N_GATE_ROWS = 32

VMEM_LIMIT_BYTES = 56 * 1024 * 1024
LANES = 128

_NT = (((1,), (1,)), ((), ()))
_TN = (((0,), (0,)), ((), ()))


def _params(n_axes):
    return pltpu.CompilerParams(dimension_semantics=("arbitrary",) * n_axes,
                                vmem_limit_bytes=VMEM_LIMIT_BYTES)


def _mm_kernel(x_ref, w_ref, o_ref):
    o_ref[...] = jnp.dot(x_ref[...], w_ref[...], preferred_element_type=F32).astype(o_ref.dtype)


def _mm(x, w, out_dtype, tm, tn, name):
    M, K = x.shape
    N = w.shape[1]
    return pl.pallas_call(
        _mm_kernel, grid=(M // tm, N // tn),
        in_specs=[pl.BlockSpec((tm, K), lambda i, j: (i, 0)),
                  pl.BlockSpec((K, tn), lambda i, j: (0, j))],
        out_specs=pl.BlockSpec((tm, tn), lambda i, j: (i, j)),
        out_shape=jax.ShapeDtypeStruct((M, N), out_dtype),
        compiler_params=_params(2), name=name)(x, w)


def _mm_nt_kernel(x_ref, wt_ref, o_ref):
    o_ref[...] = lax.dot_general(x_ref[...], wt_ref[...], _NT,
                                 preferred_element_type=F32).astype(o_ref.dtype)


def _mm_nt(x, wt, out_dtype, tm, tn, name):
    M, K = x.shape
    N = wt.shape[0]
    return pl.pallas_call(
        _mm_nt_kernel, grid=(M // tm, N // tn),
        in_specs=[pl.BlockSpec((tm, K), lambda i, j: (i, 0)),
                  pl.BlockSpec((tn, K), lambda i, j: (j, 0))],
        out_specs=pl.BlockSpec((tm, tn), lambda i, j: (i, j)),
        out_shape=jax.ShapeDtypeStruct((M, N), out_dtype),
        compiler_params=_params(2), name=name)(x, wt)


CAST_TAIL_ROWS = 32


def _cast_rows_kernel(w_ref, wn_ref, o_ref, *, shift):
    if shift:
        o_ref[...] = jnp.concatenate([w_ref[shift:, :], wn_ref[:shift, :]], axis=0).astype(o_ref.dtype)
    else:
        o_ref[...] = w_ref[...].astype(o_ref.dtype)


def _cast_rows(wt, row0, n_rows, tn=512):
    K = wt.shape[1]
    base, shift = divmod(row0, tn)
    assert shift % 8 == 0 and shift <= CAST_TAIL_ROWS and tn % CAST_TAIL_ROWS == 0
    per = tn // CAST_TAIL_ROWS
    return pl.pallas_call(
        functools.partial(_cast_rows_kernel, shift=shift), grid=(n_rows // tn,),
        in_specs=[pl.BlockSpec((tn, K), lambda j: (base + j, 0)),
                  pl.BlockSpec((CAST_TAIL_ROWS, K), lambda j: ((base + j + 1) * per, 0))],
        out_specs=pl.BlockSpec((tn, K), lambda j: (j, 0)),
        out_shape=jax.ShapeDtypeStruct((n_rows, K), BF16),
        compiler_params=_params(1), name="cast_rows")(wt, wt)


def _transpose_cast_kernel(x_ref, o_ref):
    o_ref[...] = x_ref[...].T.astype(o_ref.dtype)


def _transpose_cast(x, dtype, tr=512):
    R, C = x.shape
    return pl.pallas_call(
        _transpose_cast_kernel, grid=(R // tr,),
        in_specs=[pl.BlockSpec((tr, C), lambda i: (i, 0))],
        out_specs=pl.BlockSpec((C, tr), lambda i: (0, i)),
        out_shape=jax.ShapeDtypeStruct((C, R), dtype),
        compiler_params=_params(1), name="transpose_cast")(x)


def _gates_kernel(x_ref, wg_ref, b_ref, o_ref, carry_ref, *, ts):
    j = pl.program_id(1)

    @pl.when(j == 0)
    def _():
        carry_ref[...] = jnp.zeros_like(carry_ref)

    raw = lax.dot_general(wg_ref[...], x_ref[...], _NT, preferred_element_type=F32)
    pre = raw + b_ref[:, 0:1]
    ls = jnp.minimum(pre, 0.0) - jnp.log1p(jnp.exp(-jnp.abs(pre)))
    lane = lax.broadcasted_iota(jnp.int32, ls.shape, 1)
    c = ls
    sh = 1
    while sh < ts:
        c = c + jnp.where(lane >= sh, pltpu.roll(c, sh, axis=1), 0.0)
        sh *= 2
    c = c + carry_ref[:, 0:1]
    carry_ref[...] = jnp.broadcast_to(c[:, ts - 1:ts], carry_ref.shape)
    row = lax.broadcasted_iota(jnp.int32, ls.shape, 0)
    o_ref[0] = jnp.where(row < ML_HEADS, pre, c)


def _gates(xb, wgt, bias, B, S, ts=512):
    K = xb.shape[1]
    ns = S // ts
    return pl.pallas_call(
        functools.partial(_gates_kernel, ts=ts), grid=(B, ns),
        in_specs=[pl.BlockSpec((ts, K), lambda b, j: (b * ns + j, 0)),
                  pl.BlockSpec((N_GATE_ROWS, K), lambda b, j: (0, 0)),
                  pl.BlockSpec((N_GATE_ROWS, LANES), lambda b, j: (0, 0))],
        out_specs=pl.BlockSpec((1, N_GATE_ROWS, ts), lambda b, j: (b, 0, j)),
        out_shape=jax.ShapeDtypeStruct((B, N_GATE_ROWS, S), F32),
        scratch_shapes=[pltpu.VMEM((N_GATE_ROWS, LANES), F32)],
        compiler_params=_params(2), name="gates")(xb, wgt, bias)


def _conv_kernel(x_ref, w_ref, b_ref, sc_ref, o_ref):
    x = x_ref[...]
    rows = lax.broadcasted_iota(jnp.int32, x.shape, 0)
    y = x * w_ref[CONV_W - 1:CONV_W, :] + b_ref[...]
    for d in range(1, CONV_W):
        xs = jnp.where(rows >= d, pltpu.roll(x, d, axis=0), 0.0)
        y = y + xs * w_ref[CONV_W - 1 - d:CONV_W - d, :]
    y = y * jax.nn.sigmoid(y)
    o_ref[...] = (y * sc_ref[...]).astype(o_ref.dtype)


def _conv_silu(qk_src, conv_w, conv_b, col_scale, B, S, tc=256):
    C = conv_w.shape[1]
    return pl.pallas_call(
        _conv_kernel, grid=(B, C // tc),
        in_specs=[pl.BlockSpec((S, tc), lambda b, j: (b, j)),
                  pl.BlockSpec((CONV_W, tc), lambda b, j: (0, j)),
                  pl.BlockSpec((1, tc), lambda b, j: (0, j)),
                  pl.BlockSpec((1, tc), lambda b, j: (0, j))],
        out_specs=pl.BlockSpec((S, tc), lambda b, j: (b, j)),
        out_shape=jax.ShapeDtypeStruct((B * S, C), BF16),
        compiler_params=_params(2), name="conv_silu")(qk_src, conv_w, conv_b, col_scale)


def _row_to_col(row, eye):
    return jnp.sum(jnp.where(eye, row, 0.0), axis=1, keepdims=True)


def _mlstm_kernel(q_ref, k_ref, v_ref, og_ref, gt_ref, nw_ref, out_ref,
                  C_ref, n_ref, m_ref, g_ref, *, L):
    c = pl.program_id(1)

    @pl.when(c == 0)
    def _():
        C_ref[...] = jnp.zeros_like(C_ref)
        n_ref[...] = jnp.zeros_like(n_ref)
        m_ref[...] = jnp.zeros_like(m_ref)
        g_ref[...] = jnp.zeros_like(g_ref)

    ri = lax.broadcasted_iota(jnp.int32, (L, L), 0)
    ci = lax.broadcasted_iota(jnp.int32, (L, L), 1)
    eye = ri == ci
    causal = ci <= ri
    for h in range(ML_HEADS):
        ig = gt_ref[0, h:h + 1, :]
        gsum = gt_ref[0, ML_HEADS + h:ML_HEADS + h + 1, :]
        g0 = g_ref[h:h + 1, 0:1]
        m_prev = m_ref[h:h + 1, 0:1]
        b_row = gsum - g0
        b_last = b_row[:, L - 1:L]
        u_row = ig - b_row
        b_col = _row_to_col(b_row, eye)
        u_col = _row_to_col(u_row, eye)
        dlog = jnp.where(causal, b_col + u_row, -jnp.inf)
        a_col = b_col + m_prev
        m_t = jnp.maximum(a_col, jnp.max(dlog, axis=1, keepdims=True))
        qh = q_ref[:, h * ML_QK:(h + 1) * ML_QK]
        kh = k_ref[:, h * ML_QK:(h + 1) * ML_QK]
        vh = v_ref[:, h * ML_V:(h + 1) * ML_V]
        s = lax.dot_general(qh, kh, _NT, preferred_element_type=F32)
        p = jnp.exp(dlog - m_t) * s
        inter = jnp.exp(a_col - m_t)
        c_prev = C_ref[h]
        num = (jnp.dot(p.astype(BF16), vh, preferred_element_type=F32)
               + inter * jnp.dot(qh, c_prev.astype(BF16), preferred_element_type=F32))
        n_row = n_ref[h:h + 1, :]
        den = (jnp.sum(p, axis=1, keepdims=True)
               + inter * jnp.sum(qh.astype(F32) * n_row, axis=1, keepdims=True))
        ht = num / jnp.maximum(jnp.abs(den), jnp.exp(-m_t))
        w_end = b_last + u_col
        m_new = jnp.maximum(b_last + m_prev, jnp.max(w_end, axis=0, keepdims=True))
        decay = jnp.exp(b_last + m_prev - m_new)
        wk = jnp.exp(w_end - m_new) * kh.astype(F32)
        C_ref[h] = decay * c_prev + lax.dot_general(wk.astype(BF16), vh, _TN,
                                                    preferred_element_type=F32)
        n_ref[h:h + 1, :] = decay * n_row + jnp.sum(wk, axis=0, keepdims=True)
        m_ref[h:h + 1, :] = jnp.broadcast_to(m_new, (1, LANES))
        g_ref[h:h + 1, :] = jnp.broadcast_to(gsum[:, L - 1:L], (1, LANES))
        mu = jnp.mean(ht, axis=1, keepdims=True)
        d = ht - mu
        var = jnp.mean(d * d, axis=1, keepdims=True)
        hn = d * lax.rsqrt(var + LN_EPS) * nw_ref[:, h * ML_V:(h + 1) * ML_V]
        og = og_ref[:, h * ML_V:(h + 1) * ML_V]
        out_ref[:, h * ML_V:(h + 1) * ML_V] = (hn * jax.nn.sigmoid(og)).astype(out_ref.dtype)


def _mlstm(qk, vb_src, v_col, og_src, og_col, gt, norm_w, B, S, L=128):
    nc = S // L
    qw = ML_HEADS * ML_QK
    vw = ML_HEADS * ML_V
    return pl.pallas_call(
        functools.partial(_mlstm_kernel, L=L), grid=(B, nc),
        in_specs=[pl.BlockSpec((L, qw), lambda b, c: (b * nc + c, 0)),
                  pl.BlockSpec((L, qw), lambda b, c: (b * nc + c, 1)),
                  pl.BlockSpec((L, vw), lambda b, c: (b * nc + c, v_col)),
                  pl.BlockSpec((L, vw), lambda b, c: (b * nc + c, og_col)),
                  pl.BlockSpec((1, N_GATE_ROWS, L), lambda b, c: (b, 0, c)),
                  pl.BlockSpec((1, vw), lambda b, c: (0, 0))],
        out_specs=pl.BlockSpec((L, vw), lambda b, c: (b * nc + c, 0)),
        out_shape=jax.ShapeDtypeStruct((B * S, vw), BF16),
        scratch_shapes=[pltpu.VMEM((ML_HEADS, ML_QK, ML_V), F32),
                        pltpu.VMEM((8, ML_QK), F32),
                        pltpu.VMEM((8, LANES), F32),
                        pltpu.VMEM((8, LANES), F32)],
        compiler_params=_params(2), name="mlstm")(qk, qk, vb_src, og_src, gt, norm_w)


def _fox_kernel(q_ref, k_ref, v_ref, f_ref, o_ref, *, tq):
    S = q_ref.shape[0]
    scale = FOX_HD ** -0.5
    ri = lax.broadcasted_iota(jnp.int32, (tq, tq), 0)
    ci = lax.broadcasted_iota(jnp.int32, (tq, tq), 1)
    causal = ci <= ri
    for i in range(S // tq):
        lo = i * tq
        q = q_ref[lo:lo + tq, :]
        sd = lax.dot_general(q, k_ref[lo:lo + tq, :], _NT, preferred_element_type=F32) * scale
        sd = jnp.where(causal, sd - f_ref[0, 0, :, lo:lo + tq], -jnp.inf)
        m = jnp.max(sd, axis=1, keepdims=True)
        if i > 0:
            sl = lax.dot_general(q, k_ref[0:lo, :], _NT, preferred_element_type=F32) * scale
            sl = sl - f_ref[0, 0, :, 0:lo]
            m = jnp.maximum(m, jnp.max(sl, axis=1, keepdims=True))
        pd = jnp.exp(sd - m)
        l = jnp.sum(pd, axis=1, keepdims=True)
        o = jnp.dot(pd.astype(BF16), v_ref[lo:lo + tq, :], preferred_element_type=F32)
        if i > 0:
            pl_ = jnp.exp(sl - m)
            l = l + jnp.sum(pl_, axis=1, keepdims=True)
            o = o + jnp.dot(pl_.astype(BF16), v_ref[0:lo, :], preferred_element_type=F32)
        o_ref[lo:lo + tq, :] = (o / l).astype(o_ref.dtype)


def _fox(src, q_col, k_col, v_col, gt4, B, S, tq=256):
    frow = 2 * ML_HEADS
    return pl.pallas_call(
        functools.partial(_fox_kernel, tq=tq), grid=(B, FOX_HEADS),
        in_specs=[pl.BlockSpec((S, FOX_HD), lambda b, h: (b, q_col + h)),
                  pl.BlockSpec((S, FOX_HD), lambda b, h: (b, k_col + h)),
                  pl.BlockSpec((S, FOX_HD), lambda b, h: (b, v_col + h)),
                  pl.BlockSpec((1, 1, 1, S), lambda b, h: (b, frow + h, 0, 0))],
        out_specs=pl.BlockSpec((S, FOX_HD), lambda b, h: (b, h)),
        out_shape=jax.ShapeDtypeStruct((B * S, FOX_HEADS * FOX_HD), BF16),
        compiler_params=_params(2), name="fox_attn")(src, src, src, gt4)


def _merge_kernel(h_ref, y_ref, wm_ref, wf_ref, gm_ref, gf_ref, o_ref):
    a = jnp.dot(h_ref[...], wm_ref[...], preferred_element_type=F32)
    b = jnp.dot(y_ref[...], wf_ref[...], preferred_element_type=F32)
    o_ref[...] = (jax.nn.sigmoid(gm_ref[...]) * a + jax.nn.sigmoid(gf_ref[...]) * b).astype(o_ref.dtype)


def _merge(h, y, wm, wf, g_src, gm_col, gf_col, tm=1024, tn=512):
    M, K = h.shape
    N = wm.shape[1]
    nb = N // tn
    return pl.pallas_call(
        _merge_kernel, grid=(M // tm, nb),
        in_specs=[pl.BlockSpec((tm, K), lambda i, j: (i, 0)),
                  pl.BlockSpec((tm, K), lambda i, j: (i, 0)),
                  pl.BlockSpec((K, tn), lambda i, j: (0, j)),
                  pl.BlockSpec((K, tn), lambda i, j: (0, j)),
                  pl.BlockSpec((tm, tn), lambda i, j: (i, gm_col * nb + j)),
                  pl.BlockSpec((tm, tn), lambda i, j: (i, gf_col * nb + j))],
        out_specs=pl.BlockSpec((tm, tn), lambda i, j: (i, j)),
        out_shape=jax.ShapeDtypeStruct((M, N), BF16),
        compiler_params=_params(2), name="merge")(h, y, wm, wf, g_src, g_src)


def _ln(z, w, b):
    mu = jnp.mean(z, axis=-1, keepdims=True)
    d = z - mu
    var = jnp.mean(d * d, axis=-1, keepdims=True)
    return d * lax.rsqrt(var + LN_EPS) * w + b


def _mm_ln_kernel(a_ref, w_ref, x_ref, lw_ref, lb_ref, o_ref, ob_ref, *, alpha):
    hproj = jnp.dot(a_ref[...], w_ref[...], preferred_element_type=F32)
    out = _ln(alpha * x_ref[...] + hproj, lw_ref[...], lb_ref[...])
    o_ref[...] = out
    ob_ref[...] = out.astype(BF16)


def _mm_ln(a, w, xres, lw, lb, alpha, tm=512):
    M, K = a.shape
    N = w.shape[1]
    return pl.pallas_call(
        functools.partial(_mm_ln_kernel, alpha=alpha), grid=(M // tm,),
        in_specs=[pl.BlockSpec((tm, K), lambda i: (i, 0)),
                  pl.BlockSpec((K, N), lambda i: (0, 0)),
                  pl.BlockSpec((tm, N), lambda i: (i, 0)),
                  pl.BlockSpec((1, N), lambda i: (0, 0)),
                  pl.BlockSpec((1, N), lambda i: (0, 0))],
        out_specs=[pl.BlockSpec((tm, N), lambda i: (i, 0)),
                   pl.BlockSpec((tm, N), lambda i: (i, 0))],
        out_shape=[jax.ShapeDtypeStruct((M, N), F32), jax.ShapeDtypeStruct((M, N), BF16)],
        compiler_params=_params(1), name="proj_ln")(a, w, xres, lw, lb)


def _add_ln_kernel(ht_ref, x_ref, lw_ref, lb_ref, o_ref, *, alpha):
    o_ref[...] = _ln(alpha * x_ref[...] + ht_ref[...].T, lw_ref[...], lb_ref[...])


def _add_ln(ht, xres, lw, lb, alpha, tm=512):
    N, M = ht.shape
    return pl.pallas_call(
        functools.partial(_add_ln_kernel, alpha=alpha), grid=(M // tm,),
        in_specs=[pl.BlockSpec((N, tm), lambda i: (0, i)),
                  pl.BlockSpec((tm, N), lambda i: (i, 0)),
                  pl.BlockSpec((1, N), lambda i: (0, 0)),
                  pl.BlockSpec((1, N), lambda i: (0, 0))],
        out_specs=pl.BlockSpec((tm, N), lambda i: (i, 0)),
        out_shape=jax.ShapeDtypeStruct((M, N), F32),
        compiler_params=_params(1), name="add_ln")(ht, xres, lw, lb)


def _xattn_kernel(q_ref, kv_ref, o_ref, *, hd, d_model):
    scale = hd ** -0.5
    for h in range(X_HEADS):
        qh = q_ref[:, h * hd:(h + 1) * hd]
        kh = kv_ref[:, h * hd:(h + 1) * hd]
        vh = kv_ref[:, d_model + h * hd:d_model + (h + 1) * hd]
        s = lax.dot_general(qh, kh, _NT, preferred_element_type=F32) * scale
        m = jnp.max(s, axis=1, keepdims=True)
        p = jnp.exp(s - m)
        l = jnp.sum(p, axis=1, keepdims=True)
        o = jnp.dot(p.astype(BF16), vh, preferred_element_type=F32) / l
        o_ref[:, h * hd:(h + 1) * hd] = o.astype(o_ref.dtype)


def _xattn(q, kv, B, S, mem_len, tq=512):
    D = q.shape[1]
    nq = S // tq
    return pl.pallas_call(
        functools.partial(_xattn_kernel, hd=D // X_HEADS, d_model=D), grid=(B, nq),
        in_specs=[pl.BlockSpec((tq, D), lambda b, i: (b * nq + i, 0)),
                  pl.BlockSpec((mem_len, 2 * D), lambda b, i: (b, 0))],
        out_specs=pl.BlockSpec((tq, D), lambda b, i: (b * nq + i, 0)),
        out_shape=jax.ShapeDtypeStruct((B * S, D), BF16),
        compiler_params=_params(2), name="xattn")(q, kv)


def _top_values(work, k):
    vals = []
    for _ in range(k):
        m = jnp.max(work, axis=0, keepdims=True)
        vals.append(m)
        work = jnp.where(work == m, -jnp.inf, work)
    return vals


def _stack_rows(rows, n):
    ri = lax.broadcasted_iota(jnp.int32, (n, rows[0].shape[1]), 0)
    out = jnp.zeros((n, rows[0].shape[1]), F32)
    for r, v in enumerate(rows):
        out = jnp.where(ri == r, v, out)
    return out


def _pair_sums(t1, t2):
    t1s = _stack_rows(t1, PEER_TOPK)
    t2s = _stack_rows(t2, PEER_TOPK)
    r8 = lax.broadcasted_iota(jnp.int32, (8, t1s.shape[1]), 0)
    ninf = -jnp.inf
    pieces = [t1[0] + t2s,
              t1[1] + t2s[0:8],
              t1s[8:16] + t2[0],
              jnp.where(r8 < 5, t1[2] + t2s[0:8], ninf),
              jnp.where(r8 < 4, t1[3] + t2s[0:8], ninf),
              jnp.where(r8 < 3, t1[4] + t2s[0:8], ninf),
              jnp.where(r8 >= 5, t1s[0:8] + t2[0], ninf),
              jnp.where(r8 >= 5, t1s[0:8] + t2[1], ninf)]
    return jnp.concatenate(pieces, axis=0)


def _peer_score_kernel(q_ref, keys_ref, s1_ref, s2_ref, st_ref):
    dh = q_ref.shape[1] // (2 * PEER_HEADS)
    log2e = 1.0 / math.log(2.0)
    for h in range(PEER_HEADS):
        q1 = q_ref[:, (2 * h) * dh:(2 * h + 1) * dh]
        q2 = q_ref[:, (2 * h + 1) * dh:(2 * h + 2) * dh]
        s1 = lax.dot_general(keys_ref[2 * h], q1, _NT, preferred_element_type=F32) * log2e
        s2 = lax.dot_general(keys_ref[2 * h + 1], q2, _NT, preferred_element_type=F32) * log2e
        t1 = _top_values(s1, PEER_TOPK)
        t2 = _top_values(s2, PEER_TOPK)
        best = _top_values(_pair_sums(t1, t2), PEER_TOPK)
        m = best[0]
        z = jnp.zeros_like(m)
        for r in range(PEER_TOPK):
            z = z + jnp.exp2(best[r] - m)
        mz = m + jnp.log2(z) + 1.0
        tau = _top_values(_pair_sums([t - mz for t in t1], t2), PEER_TOPK)[PEER_TOPK - 1]
        s1_ref[h] = s1 - mz
        s2_ref[h] = s2
        st_ref[h] = jnp.broadcast_to(tau, (8, tau.shape[1]))


def _peer_scores(pq, keys, tm=256):
    M, W = pq.shape
    return pl.pallas_call(
        _peer_score_kernel, grid=(M // tm,),
        in_specs=[pl.BlockSpec((tm, W), lambda i: (i, 0)),
                  pl.BlockSpec(keys.shape, lambda i: (0, 0, 0))],
        out_specs=[pl.BlockSpec((PEER_HEADS, N_KEYS, tm), lambda i: (0, 0, i)),
                   pl.BlockSpec((PEER_HEADS, N_KEYS, tm), lambda i: (0, 0, i)),
                   pl.BlockSpec((PEER_HEADS, 8, tm), lambda i: (0, 0, i))],
        out_shape=[jax.ShapeDtypeStruct((PEER_HEADS, N_KEYS, M), F32),
                   jax.ShapeDtypeStruct((PEER_HEADS, N_KEYS, M), F32),
                   jax.ShapeDtypeStruct((PEER_HEADS, 8, M), F32)],
        compiler_params=_params(1), name="peer_scores")(pq, keys)


def _peer_mix_kernel(xt_ref, u_ref, vt_ref, s1_ref, s2_ref, st_ref, o_ref, a_sc, h_sc, *, ti, n_tiles, tn):
    g = pl.program_id(1)
    te = ti * N_KEYS

    @pl.when(g == 0)
    def _():
        o_ref[...] = jnp.zeros_like(o_ref)
        a_sc[...] = jnp.zeros_like(a_sc)
        h_sc[...] = jnp.zeros_like(h_sc)

    def stage_a(half, tok):
        return jnp.dot(u_ref[half * te:(half + 1) * te, :], xt_ref[:, tok],
                       preferred_element_type=F32)

    def rep(rows8):
        return jnp.concatenate([rows8] * (N_KEYS // 8), axis=0)

    def stage_b(a_t, tile, tok):
        tile = jnp.clip(tile, 0, n_tiles - 1)
        rows = []
        for il in range(ti):
            i = tile * ti + il
            acc = None
            for h in range(PEER_HEADS):
                d = s1_ref[h, pl.ds(i, 1), tok] + s2_ref[h, :, tok]
                w = jnp.where(d >= rep(st_ref[h, :, tok]), jnp.exp2(d), 0.0)
                acc = w if acc is None else acc + w
            rows.append(acc)
        half_gate = jnp.concatenate(rows, axis=0)
        erf_term = 1.0 + lax.erf(a_t * (1.0 / math.sqrt(2.0)))
        return (a_t * erf_term * half_gate).astype(BF16)

    for tb in range(xt_ref.shape[1] // tn):
        tok = slice(tb * tn, (tb + 1) * tn)
        a_even = stage_a(0, tok)
        h_odd = stage_b(a_sc[:, tok], 2 * g - 1, tok)
        h_pair = jnp.concatenate([h_sc[:, tok], h_odd], axis=0)
        o_ref[:, tok] += jnp.dot(vt_ref[...], h_pair, preferred_element_type=F32)
        a_sc[:, tok] = stage_a(1, tok)
        h_sc[:, tok] = stage_b(a_even, 2 * g, tok)


def _peer_mix(xbt, u, vt, s1, s2, st, tm=512, ti=4, tn=256):
    D, M = xbt.shape
    E = u.shape[0]
    te = ti * N_KEYS
    n_tiles = E // te
    n_pairs = n_tiles // 2
    return pl.pallas_call(
        functools.partial(_peer_mix_kernel, ti=ti, n_tiles=n_tiles, tn=tn), grid=(M // tm, n_pairs + 1),
        in_specs=[pl.BlockSpec((D, tm), lambda i, g: (0, i)),
                  pl.BlockSpec((2 * te, D), lambda i, g: (jnp.minimum(g, n_pairs - 1), 0)),
                  pl.BlockSpec((D, 2 * te), lambda i, g: (0, jnp.maximum(g - 1, 0))),
                  pl.BlockSpec((PEER_HEADS, N_KEYS, tm), lambda i, g: (0, 0, i)),
                  pl.BlockSpec((PEER_HEADS, N_KEYS, tm), lambda i, g: (0, 0, i)),
                  pl.BlockSpec((PEER_HEADS, 8, tm), lambda i, g: (0, 0, i))],
        out_specs=pl.BlockSpec((D, tm), lambda i, g: (0, i)),
        out_shape=jax.ShapeDtypeStruct((D, M), F32),
        scratch_shapes=[pltpu.VMEM((te, tm), F32), pltpu.VMEM((te, tm), BF16)],
        compiler_params=_params(2), name="peer_mix")(xbt, u, vt, s1, s2, st)


def _layer(xf, xb, mem_b, p, B, S, alpha):
    D = xf.shape[1]
    w_in = p["w_in"]
    qw = ML_HEADS * ML_QK
    vw = ML_HEADS * ML_V
    fw = FOX_HEADS * FOX_HD
    o = [0]
    for width in (qw, qw, vw, vw, ML_HEADS, ML_HEADS, fw, fw, fw, FOX_HEADS, D, D):
        o.append(o[-1] + width)
    w_in_t = w_in.T
    seg = lambda a, b: _cast_rows(w_in_t, o[a], o[b] - o[a])
    p_qk = _mm_nt(xb, seg(0, 2), F32, 1024, 1024, "in_proj_qk")
    p_v = _mm_nt(xb, seg(2, 3), BF16, 1024, 1024, "in_proj_v")
    p_o = _mm_nt(xb, seg(3, 4), F32, 1024, 1024, "in_proj_o")
    p_fx = _mm_nt(xb, seg(6, 9), BF16, 1024, 1024, "in_proj_fox")
    p_g = _mm_nt(xb, seg(10, 12), F32, 1024, 1024, "in_proj_gate")
    n_g = 2 * ML_HEADS + FOX_HEADS
    ml_lo, fx_lo = (o[4] // LANES) * LANES, (o[9] // LANES) * LANES
    ml_win = _cast_rows(w_in_t, ml_lo, LANES, tn=LANES)
    fx_win = _cast_rows(w_in_t, fx_lo, LANES, tn=LANES)
    wgt = jnp.concatenate([ml_win[o[4] - ml_lo:o[6] - ml_lo], fx_win[o[9] - fx_lo:o[10] - fx_lo]], axis=0)
    wgt = jnp.pad(wgt, ((0, N_GATE_ROWS - n_g), (0, 0)))
    gbias = jnp.concatenate([p["ml_gate_b"][0], p["ml_gate_b"][1], p["fox_f_b"],
                             jnp.zeros((N_GATE_ROWS - n_g,), F32)]).astype(F32)
    gbias = jnp.broadcast_to(gbias[:, None], (N_GATE_ROWS, LANES))
    gt = _gates(xb, wgt, gbias, B, S)

    col_scale = jnp.concatenate([jnp.ones((qw,), F32), jnp.full((qw,), ML_QK ** -0.5, F32)])[None, :]
    qk = _conv_silu(p_qk, p["ml_conv_w"], p["ml_conv_b"][None, :], col_scale, B, S)
    hg = _mlstm(qk, p_v, 0, p_o, 0, gt, p["ml_norm_w"][None, :], B, S)
    nfb = fw // FOX_HD
    y = _fox(p_fx, 0, nfb, 2 * nfb, gt.reshape(B, N_GATE_ROWS, 1, S), B, S)
    merged = _merge(hg, y, p["w_branch_ml"].astype(BF16), p["w_branch_fox"].astype(BF16), p_g, 0, 1)
    x1, x1b = _mm_ln(merged, p["w_out"].astype(BF16), xf, p["ln_w"][0:1], p["ln_b"][0:1], alpha)

    xq = _mm(x1b, p["w_xq"].astype(BF16), BF16, 1024, 512, "xattn_q")
    mem_len = mem_b.shape[0] // B
    kv = _mm(mem_b, p["w_xkv"].astype(BF16), BF16, 512, 512, "xattn_kv")
    xo = _xattn(xq, kv, B, S, mem_len)
    x2, x2b = _mm_ln(xo, p["w_xo"].astype(BF16), x1, p["ln_w"][1:2], p["ln_b"][1:2], alpha)

    pq = _mm(x2b, p["peer_wq"].astype(BF16), BF16, 1024, 512, "peer_q")
    keys = p["peer_sub_keys"].reshape(2 * PEER_HEADS, N_KEYS, -1).astype(BF16)
    s1, s2, st = _peer_scores(pq, keys)
    ht = _peer_mix(_transpose_cast(x2, BF16), p["peer_u"].astype(BF16), _transpose_cast(p["peer_v"], BF16),
                   s1, s2, st)
    x3 = _add_ln(ht, x2, p["ln_w"][2:3], p["ln_b"][2:3], alpha)
    return x3


def kernel(x, mem, w_in, ml_conv_w, ml_conv_b, ml_gate_b, ml_norm_w, fox_f_b, w_branch_ml, w_branch_fox,
           w_out, w_xq, w_xkv, w_xo, peer_wq, peer_sub_keys, peer_u, peer_v, ln_w, ln_b):
    B, S, D = x.shape
    depth = w_in.shape[0]
    alpha = (2 * depth) ** 0.25
    xf = x.reshape(B * S, D)
    mem_b = mem.reshape(-1, D).astype(BF16)
    for l in range(depth):
        p = dict(w_in=w_in[l], ml_conv_w=ml_conv_w[l], ml_conv_b=ml_conv_b[l], ml_gate_b=ml_gate_b[l],
                 ml_norm_w=ml_norm_w[l], fox_f_b=fox_f_b[l], w_branch_ml=w_branch_ml[l],
                 w_branch_fox=w_branch_fox[l], w_out=w_out[l], w_xq=w_xq[l], w_xkv=w_xkv[l], w_xo=w_xo[l],
                 peer_wq=peer_wq[l], peer_sub_keys=peer_sub_keys[l], peer_u=peer_u[l], peer_v=peer_v[l],
                 ln_w=ln_w[l], ln_b=ln_b[l])
        xf = _layer(xf, xf.astype(BF16), mem_b, p, B, S, alpha)
    return xf.reshape(B, S, D)
```
